```python
import math
import jax
import jax.numpy as jnp
from jax import lax
import numpy as np

D_MODEL = 2048
BATCH = 2
SEQ = 8192
DEPTH = 4

D_MIX = D_MODEL
GROUP_W = D_MIX // 4

S5_W = GROUP_W
S5_CH = 16
S5_GROUPS = S5_W // S5_CH
S5_STATE = 64
S5_DT_MIN = 1e-3
S5_DT_MAX = 1e-1

LRU_W = GROUP_W
LRU_HEADS = 8
LRU_HEAD_DIM = LRU_W // LRU_HEADS
LRU_C = 8.0
CONV_K = 4

RW_W = GROUP_W
RW_HEAD_DIM = 64
RW_HEADS = RW_W // RW_HEAD_DIM
RW_DECAY_LORA = 32
RW_AAA_LORA = 32
RW_GATE_LORA = 96
RW_COLS = 3 * RW_W + RW_DECAY_LORA + RW_AAA_LORA + RW_GATE_LORA
RW_LN_EPS = 64e-5

SSD_W = GROUP_W
SSD_HEAD_DIM = 64
SSD_HEADS = SSD_W // SSD_HEAD_DIM
SSD_GROUPS = 2
SSD_STATE = 128
SSD_CHUNK = 128
SSD_XBC = SSD_W + 2 * SSD_GROUPS * SSD_STATE

IN_COLS = S5_W + 2 * LRU_W + RW_COLS + SSD_W + SSD_XBC + SSD_HEADS

MOE_GROUPS = 8
MOE_PER_GROUP = 8
MOE_EXPERTS = MOE_GROUPS * MOE_PER_GROUP
MOE_TOPK = 2
MOE_DFF = 384
MOE_BLOCK = 128

NORM_EPS = 1e-6

kernel_name = 'hybrid_s5_rglru_rwkv7_ssd_hmoe'


def rms_norm(x, g):
    xf = x.astype(jnp.float32)
    y = xf * lax.rsqrt(jnp.mean(xf * xf, axis=-1, keepdims=True) + NORM_EPS)
    return (y * g.astype(jnp.float32)).astype(x.dtype)


def _split(x, sizes):
    idx = np.cumsum(np.array(sizes))[:-1].tolist()
    return jnp.split(x, idx, axis=-1)


def causal_conv(x, w, b):
    k = w.shape[0]
    s = x.shape[1]
    xp = jnp.pad(x, ((0, 0), (k - 1, 0), (0, 0)))
    y = xp[:, 0:s] * w[0]
    for i in range(1, k):
        y = y + xp[:, i:i + s] * w[i]
    return y + b


def token_shift(x):
    return jnp.pad(x, ((0, 0), (1, 0), (0, 0)))[:, :-1]


def _complex_affine_combine(e1, e2):
    a1r, a1i, b1r, b1i = e1
    a2r, a2i, b2r, b2i = e2
    return (a2r * a1r - a2i * a1i,
            a2r * a1i + a2i * a1r,
            a2r * b1r - a2i * b1i + b2r,
            a2r * b1i + a2i * b1r + b2i)


def _affine_combine(e1, e2):
    a1, b1 = e1
    a2, b2 = e2
    return a1 * a2, a2 * b1 + b2


def s5_mixer(u, lam_re, lam_im, b_re, b_im, c_re, c_im, d_skip, log_dt, w_glu, b_glu, gain):
    f32 = jnp.float32
    bsz, s, _ = u.shape
    uf = u.astype(f32)
    dt = jnp.exp(log_dt.astype(f32))[:, None]
    lr, li = lam_re.astype(f32), lam_im.astype(f32)
    mag = jnp.exp(lr * dt)
    abar_re, abar_im = mag * jnp.cos(li * dt), mag * jnp.sin(li * dt)
    den = lr * lr + li * li
    q_re = ((abar_re - 1.0) * lr + abar_im * li) / den
    q_im = (abar_im * lr - (abar_re - 1.0) * li) / den
    br, bi = b_re.astype(f32), b_im.astype(f32)
    bbar_re = q_re[..., None] * br - q_im[..., None] * bi
    bbar_im = q_re[..., None] * bi + q_im[..., None] * br
    ug = uf.reshape(bsz, s, S5_GROUPS, S5_CH)
    bu_re = jnp.einsum('bsgh,gnh->bsgn', ug, bbar_re)
    bu_im = jnp.einsum('bsgh,gnh->bsgn', ug, bbar_im)
    a_re = jnp.broadcast_to(abar_re, bu_re.shape)
    a_im = jnp.broadcast_to(abar_im, bu_re.shape)
    _, _, st_re, st_im = lax.associative_scan(
        _complex_affine_combine, (a_re, a_im, bu_re, bu_im), axis=1)
    y = (jnp.einsum('bsgn,ghn->bsgh', st_re, c_re.astype(f32))
         - jnp.einsum('bsgn,ghn->bsgh', st_im, c_im.astype(f32)))
    y = y.reshape(bsz, s, S5_W) + d_skip.astype(f32) * uf
    y = jax.nn.gelu(y)
    y = y * jax.nn.sigmoid(y @ w_glu.astype(f32) + b_glu.astype(f32))
    return rms_norm(y, gain).astype(u.dtype)


def rglru_mixer(xb, gate, conv_w, conv_b, wa, ba, wx, bx, a_param, gain):
    f32 = jnp.float32
    bsz, s, _ = xb.shape
    xc = causal_conv(xb.astype(f32), conv_w.astype(f32), conv_b.astype(f32))
    xh = xc.reshape(bsz, s, LRU_HEADS, LRU_HEAD_DIM)
    r = jax.nn.sigmoid(jnp.einsum('bshi,hij->bshj', xh, wa.astype(f32)).reshape(bsz, s, LRU_W) + ba.astype(f32))
    i = jax.nn.sigmoid(jnp.einsum('bshi,hij->bshj', xh, wx.astype(f32)).reshape(bsz, s, LRU_W) + bx.astype(f32))
    log_a = -LRU_C * r * jax.nn.softplus(-a_param.astype(f32))
    a = jnp.exp(log_a)
    u = jnp.sqrt(-jnp.expm1(2.0 * log_a)) * (i * xc)
    _, hstate = lax.associative_scan(_affine_combine, (a, u), axis=1)
    y = hstate * jax.nn.gelu(gate.astype(f32))
    return rms_norm(y, gain).astype(xb.dtype)


def rwkv7_mixer(p, mu, w0, w2, a0, a2, g2, k_k, k_a, r_k, ln_w, ln_b):
    f32 = jnp.float32
    bsz, s, _ = p.shape
    pf = p.astype(f32)
    pf = pf + (token_shift(pf) - pf) * mu.astype(f32)
    r, k, v, xw, xa, xg = _split(pf, [RW_W, RW_W, RW_W, RW_DECAY_LORA, RW_AAA_LORA, RW_GATE_LORA])
    w_log = -jax.nn.softplus(-(w0.astype(f32) + jnp.tanh(xw) @ w2.astype(f32))) - 0.5
    decay = jnp.exp(-jnp.exp(w_log))
    a = jax.nn.sigmoid(a0.astype(f32) + xa @ a2.astype(f32))
    g = jax.nn.sigmoid(xg) @ g2.astype(f32)
    kk = (k * k_k.astype(f32)).reshape(bsz, s, RW_HEADS, RW_HEAD_DIM)
    kk = kk * lax.rsqrt(jnp.maximum(jnp.sum(kk * kk, axis=-1, keepdims=True), 1e-24))
    k = k * (1.0 + (a - 1.0) * k_a.astype(f32))

    def heads(t):
        return t.reshape(bsz, s, RW_HEADS, RW_HEAD_DIM)

    r_h, w_h, k_h, v_h, a_h = heads(r), heads(decay), heads(k), heads(v), heads(a)
    xs = tuple(jnp.moveaxis(t, 1, 0) for t in (r_h, w_h, k_h, v_h, kk, a_h))

    def step(state, inp):
        r_t, w_t, k_t, v_t, kk_t, a_t = inp
        sa = jnp.einsum('bhvk,bhk->bhv', state, -kk_t)
        state = (state * w_t[:, :, None, :]
                 + sa[..., None] * (kk_t * a_t)[:, :, None, :]
                 + v_t[..., None] * k_t[:, :, None, :])
        return state, jnp.einsum('bhvk,bhk->bhv', state, r_t)

    s0 = jnp.zeros((bsz, RW_HEADS, RW_HEAD_DIM, RW_HEAD_DIM), f32)
    _, o = lax.scan(step, s0, xs)
    o = jnp.moveaxis(o, 0, 1)
    mean = jnp.mean(o, axis=-1, keepdims=True)
    var = jnp.mean(jnp.square(o - mean), axis=-1, keepdims=True)
    o = ((o - mean) * lax.rsqrt(var + RW_LN_EPS)).reshape(bsz, s, RW_W)
    o = o * ln_w.astype(f32) + ln_b.astype(f32)
    bonus = jnp.sum(r_h * k_h * r_k.astype(f32), axis=-1, keepdims=True) * v_h
    y = (o + bonus.reshape(bsz, s, RW_W)) * g
    return y.astype(p.dtype)


def ssd_chunked(x, a, bmat, cmat):
    bsz, s, _, _ = x.shape
    nc, ln, hg = s // SSD_CHUNK, SSD_CHUNK, SSD_HEADS // SSD_GROUPS
    x = x.reshape(bsz, nc, ln, SSD_GROUPS, hg, SSD_HEAD_DIM)
    a = a.reshape(bsz, nc, ln, SSD_GROUPS, hg)
    bmat = bmat.reshape(bsz, nc, ln, SSD_GROUPS, SSD_STATE)
    cmat = cmat.reshape(bsz, nc, ln, SSD_GROUPS, SSD_STATE)
    a_cs = jnp.cumsum(a, axis=2)
    seg = a_cs[:, :, :, None] - a_cs[:, :, None, :]
    mask = jnp.tril(jnp.ones((ln, ln), dtype=bool))[None, None, :, :, None, None]
    decay = jnp.exp(jnp.where(mask, seg, -jnp.inf))
    scores = jnp.einsum('bclgn,bcsgn->bclsg', cmat, bmat)
    y_diag = jnp.einsum('bclsgh,bcsghp->bclghp', scores[..., None] * decay, x)
    decay_to_end = jnp.exp(a_cs[:, :, -1:] - a_cs)
    states = jnp.einsum('bclgn,bclghp->bcghpn', bmat, x * decay_to_end[..., None])
    chunk_decay = jnp.exp(a_cs[:, :, -1])

    def chunk_step(h, inp):
        st, dec = inp
        return h * dec[..., None, None] + st, h

    h0 = jnp.zeros((bsz, SSD_GROUPS, hg, SSD_HEAD_DIM, SSD_STATE), x.dtype)
    _, h_in = lax.scan(chunk_step, h0, (jnp.moveaxis(states, 1, 0), jnp.moveaxis(chunk_decay, 1, 0)))
    h_in = jnp.moveaxis(h_in, 0, 1)
    y_off = jnp.einsum('bclgn,bcghpn->bclghp', cmat, h_in) * jnp.exp(a_cs)[..., None]
    return (y_diag + y_off).reshape(bsz, s, SSD_HEADS, SSD_HEAD_DIM)


def ssd_mixer(z, xbc, dt_raw, conv_w, conv_b, dt_bias, a_log, d_skip, norm_w):
    f32 = jnp.float32
    bsz, s, _ = z.shape
    xbc = jax.nn.silu(causal_conv(xbc.astype(f32), conv_w.astype(f32), conv_b.astype(f32)))
    xs, bmat, cmat = _split(xbc, [SSD_W, SSD_GROUPS * SSD_STATE, SSD_GROUPS * SSD_STATE])
    xh = xs.reshape(bsz, s, SSD_HEADS, SSD_HEAD_DIM)
    dt = jax.nn.softplus(dt_raw.astype(f32) + dt_bias.astype(f32))
    a_coef = -jnp.exp(a_log.astype(f32))
    y = ssd_chunked(xh * dt[..., None], dt * a_coef,
                    bmat.reshape(bsz, s, SSD_GROUPS, SSD_STATE),
                    cmat.reshape(bsz, s, SSD_GROUPS, SSD_STATE))
    y = (y + xh * d_skip.astype(f32)[:, None]).reshape(bsz, s, SSD_W)
    y = rms_norm(y * jax.nn.silu(z.astype(f32)), norm_w)
    return y.astype(z.dtype)


def hier_moe(h, w_group, b_group, w_expert, b_expert, w1, w3, w2):
    bsz, s, d = h.shape
    n = bsz * s
    xt = h.reshape(n, d)
    g_logits = (xt @ w_group + b_group).astype(jnp.float32)
    g_probs = jax.nn.softmax(g_logits, axis=-1)
    g_idx = jnp.argmax(g_logits, axis=-1)
    g_gate = jnp.take_along_axis(g_probs, g_idx[:, None], axis=-1)
    e_logits = (xt @ w_expert + b_expert).astype(jnp.float32).reshape(n, MOE_GROUPS, MOE_PER_GROUP)
    e_logits = jnp.take_along_axis(e_logits, g_idx[:, None, None], axis=1)[:, 0]
    top_v, top_i = lax.top_k(e_logits, MOE_TOPK)
    gate = g_gate * jax.nn.softmax(top_v, axis=-1)
    expert_id = g_idx[:, None] * MOE_PER_GROUP + top_i
    na = n * MOE_TOPK
    eid = expert_id.reshape(na)
    tok = jnp.repeat(jnp.arange(n), MOE_TOPK)
    wgt = gate.reshape(na)
    order = jnp.argsort(eid)
    eid_s, tok_s, wgt_s = eid[order], tok[order], wgt[order]
    counts = jnp.bincount(eid, length=MOE_EXPERTS)
    starts = jnp.cumsum(counts) - counts
    padded = (counts + MOE_BLOCK - 1) // MOE_BLOCK * MOE_BLOCK
    pends = jnp.cumsum(padded)
    pstarts = pends - padded
    dest = pstarts[eid_s] + (jnp.arange(na) - starts[eid_s])
    cap = -(-(na + MOE_EXPERTS * (MOE_BLOCK - 1)) // MOE_BLOCK) * MOE_BLOCK
    nb = cap // MOE_BLOCK
    xpad = jnp.zeros((cap, d), h.dtype).at[dest].set(xt[tok_s])
    blk_e = jnp.minimum(jnp.searchsorted(pends, jnp.arange(nb) * MOE_BLOCK, side='right'), MOE_EXPERTS - 1)

    def block_ffn(args):
        xb, e = args
        return (jax.nn.silu(xb @ w1[e]) * (xb @ w3[e])) @ w2[e]

    ypad = lax.map(block_ffn, (xpad.reshape(nb, MOE_BLOCK, d), blk_e)).reshape(cap, d)
    y = jnp.zeros((n, d), h.dtype).at[tok_s].add(ypad[dest] * wgt_s[:, None].astype(h.dtype))
    return y.reshape(bsz, s, d)


def setup_inputs(seed: int = 0) -> dict:
    key = jax.random.key(seed)
    keys = iter(list(jax.random.split(key, 96)))
    f32 = jnp.float32
    L = DEPTH

    def nrm(shape, scale):
        return scale * jax.random.normal(next(keys), shape, f32)

    def unif(shape, lo, hi):
        return jax.random.uniform(next(keys), shape, f32, lo, hi)

    def gain(shape):
        return 1.0 + nrm(shape, 0.02)

    lru_a = unif((L, LRU_W), 0.9, 0.999) ** (1.0 / LRU_C)
    ssd_dt = jnp.exp(unif((L, SSD_HEADS), math.log(1e-3), math.log(1e-1)))
    return {
        'x': nrm((BATCH, SEQ, D_MODEL), 1.0),
        'norm_mix': gain((L, D_MODEL)),
        'w_in': nrm((L, D_MODEL, IN_COLS), D_MODEL ** -0.5),
        'w_out': nrm((L, D_MIX, D_MODEL), 0.5 * D_MIX ** -0.5),
        's5_lambda_re': -0.5 + nrm((L, S5_GROUPS, S5_STATE), 0.01),
        's5_lambda_im': jnp.broadcast_to(math.pi * jnp.arange(S5_STATE, dtype=f32), (L, S5_GROUPS, S5_STATE)),
        's5_b_re': nrm((L, S5_GROUPS, S5_STATE, S5_CH), (2 * S5_CH) ** -0.5),
        's5_b_im': nrm((L, S5_GROUPS, S5_STATE, S5_CH), (2 * S5_CH) ** -0.5),
        's5_c_re': nrm((L, S5_GROUPS, S5_CH, S5_STATE), 2.0 * S5_STATE ** -0.5),
        's5_c_im': nrm((L, S5_GROUPS, S5_CH, S5_STATE), 2.0 * S5_STATE ** -0.5),
        's5_d': nrm((L, S5_W), 0.5),
        's5_log_dt': unif((L, S5_GROUPS), math.log(S5_DT_MIN), math.log(S5_DT_MAX)),
        's5_w_glu': nrm((L, S5_W, S5_W), S5_W ** -0.5),
        's5_b_glu': nrm((L, S5_W), 0.01),
        's5_gain': gain((L, S5_W)),
        'lru_conv_w': nrm((L, CONV_K, LRU_W), CONV_K ** -0.5),
        'lru_conv_b': nrm((L, LRU_W), 0.01),
        'lru_wa': nrm((L, LRU_HEADS, LRU_HEAD_DIM, LRU_HEAD_DIM), LRU_HEAD_DIM ** -0.5),
        'lru_ba': nrm((L, LRU_W), 0.01),
        'lru_wx': nrm((L, LRU_HEADS, LRU_HEAD_DIM, LRU_HEAD_DIM), LRU_HEAD_DIM ** -0.5),
        'lru_bx': nrm((L, LRU_W), 0.01),
        'lru_a_param': jnp.log(lru_a) - jnp.log1p(-lru_a),
        'lru_gain': gain((L, LRU_W)),
        'rw_mu': unif((L, RW_COLS), 0.0, 1.0),
        'rw_w0': unif((L, RW_W), -6.0, 1.0),
        'rw_w2': nrm((L, RW_DECAY_LORA, RW_W), 0.1),
        'rw_a0': nrm((L, RW_W), 0.1),
        'rw_a2': nrm((L, RW_AAA_LORA, RW_W), 0.1),
        'rw_g2': nrm((L, RW_GATE_LORA, RW_W), RW_GATE_LORA ** -0.5),
        'rw_k_k': 0.85 + nrm((L, RW_W), 0.02),
        'rw_k_a': 1.0 + nrm((L, RW_W), 0.02),
        'rw_r_k': nrm((L, RW_HEADS, RW_HEAD_DIM), 0.1),
        'rw_ln_w': gain((L, RW_W)),
        'rw_ln_b': nrm((L, RW_W), 0.01),
        'ssd_conv_w': nrm((L, CONV_K, SSD_XBC), CONV_K ** -0.5),
        'ssd_conv_b': nrm((L, SSD_XBC), 0.01),
        'ssd_dt_bias': ssd_dt + jnp.log(-jnp.expm1(-ssd_dt)),
        'ssd_a_log': jnp.log(unif((L, SSD_HEADS), 1.0, 16.0)),
        'ssd_d': 1.0 + nrm((L, SSD_HEADS), 0.1),
        'ssd_norm_w': gain((L, SSD_W)),
        'norm_ffn': gain((L, D_MODEL)),
        'moe_w_group': nrm((L, D_MODEL, MOE_GROUPS), D_MODEL ** -0.5),
        'moe_b_group': nrm((L, MOE_GROUPS), 0.01),
        'moe_w_expert': nrm((L, D_MODEL, MOE_EXPERTS), D_MODEL ** -0.5),
        'moe_b_expert': nrm((L, MOE_EXPERTS), 0.01),
        'moe_w1': nrm((L, MOE_EXPERTS, D_MODEL, MOE_DFF), D_MODEL ** -0.5),
        'moe_w3': nrm((L, MOE_EXPERTS, D_MODEL, MOE_DFF), D_MODEL ** -0.5),
        'moe_w2': nrm((L, MOE_EXPERTS, MOE_DFF, D_MODEL), MOE_DFF ** -0.5),
        'norm_final': gain((D_MODEL,)),
    }


def reference(x, norm_mix, w_in, w_out,
              s5_lambda_re, s5_lambda_im, s5_b_re, s5_b_im, s5_c_re, s5_c_im,
              s5_d, s5_log_dt, s5_w_glu, s5_b_glu, s5_gain,
              lru_conv_w, lru_conv_b, lru_wa, lru_ba, lru_wx, lru_bx, lru_a_param, lru_gain,
              rw_mu, rw_w0, rw_w2, rw_a0, rw_a2, rw_g2, rw_k_k, rw_k_a, rw_r_k, rw_ln_w, rw_ln_b,
              ssd_conv_w, ssd_conv_b, ssd_dt_bias, ssd_a_log, ssd_d, ssd_norm_w,
              norm_ffn, moe_w_group, moe_b_group, moe_w_expert, moe_b_expert,
              moe_w1, moe_w3, moe_w2, norm_final):
    for l in range(DEPTH):
        h = rms_norm(x, norm_mix[l])
        proj = h @ w_in[l]
        u_s5, x_lru, g_lru, p_rw, z_ssd, xbc_ssd, dt_ssd = _split(
            proj, [S5_W, LRU_W, LRU_W, RW_COLS, SSD_W, SSD_XBC, SSD_HEADS])
        y_s5 = s5_mixer(u_s5, s5_lambda_re[l], s5_lambda_im[l], s5_b_re[l], s5_b_im[l],
                        s5_c_re[l], s5_c_im[l], s5_d[l], s5_log_dt[l],
                        s5_w_glu[l], s5_b_glu[l], s5_gain[l])
        y_lru = rglru_mixer(x_lru, g_lru, lru_conv_w[l], lru_conv_b[l], lru_wa[l], lru_ba[l],
                            lru_wx[l], lru_bx[l], lru_a_param[l], lru_gain[l])
        y_rw = rwkv7_mixer(p_rw, rw_mu[l], rw_w0[l], rw_w2[l], rw_a0[l], rw_a2[l], rw_g2[l],
                           rw_k_k[l], rw_k_a[l], rw_r_k[l], rw_ln_w[l], rw_ln_b[l])
        y_ssd = ssd_mixer(z_ssd, xbc_ssd, dt_ssd, ssd_conv_w[l], ssd_conv_b[l], ssd_dt_bias[l],
                          ssd_a_log[l], ssd_d[l], ssd_norm_w[l])
        mixed = jnp.concatenate([y_s5, y_lru, y_rw, y_ssd], axis=-1)
        x = x + (mixed @ w_out[l]).astype(x.dtype)
        h = rms_norm(x, norm_ffn[l])
        x = x + hier_moe(h, moe_w_group[l], moe_b_group[l], moe_w_expert[l], moe_b_expert[l],
                         moe_w1[l], moe_w3[l], moe_w2[l]).astype(x.dtype)
    return rms_norm(x, norm_final)
```

```python
import functools

import numpy as np
import jax
import jax.numpy as jnp
from jax import lax
from jax.experimental import pallas as pl
from jax.experimental.pallas import tpu as pltpu

F32 = jnp.float32
BF16 = jnp.bfloat16
HIGHEST = lax.Precision.HIGHEST

D_MODEL = 2048
DEPTH = 4
GROUP_W = 512
NORM_EPS = 1e-6

S5_CH = 16
S5_GROUPS = 32
S5_STATE = 64
S5_NST = S5_GROUPS * S5_STATE

LRU_HEADS = 8
LRU_HEAD_DIM = 64
LRU_C = 8.0
CONV_K = 4

RW_HEADS = 8
RW_HEAD_DIM = 64
RW_DECAY_LORA = 32
RW_AAA_LORA = 32
RW_GATE_LORA = 96
RW_LORA_PAD = 256
RW_LN_EPS = 64e-5

SSD_HEADS = 8
SSD_HEAD_DIM = 64
SSD_GROUPS = 2
SSD_STATE = 128
SSD_CHUNK = 128
SSD_XBC = 1024

MOE_GROUPS = 8
MOE_PER_GROUP = 8
MOE_EXPERTS = 64
MOE_TOPK = 2
MOE_DFF = 384
MOE_BLOCK = 128

SUBLANES = 8
LANES = 128
VMEM_LIMIT = 56 * 1024 * 1024

COL_XBC = 0
COL_S5 = 1024
COL_LX = 1536
COL_LG = 2048
COL_Z = 2560
COL_R = 3072
COL_K = 3584
COL_V = 4096
COL_LORA = 4608
COL_DT = 4864
PCOLS = 4992

RW_PAIR_ORDER = (0, 4, 1, 5, 2, 6, 3, 7)


def _cparams(sem):
    return pltpu.CompilerParams(dimension_semantics=sem, vmem_limit_bytes=VMEM_LIMIT)


def _sigmoid(x):
    return 1.0 / (1.0 + jnp.exp(-x))


def _silu(x):
    return x * _sigmoid(x)


def _gelu(x):
    return 0.5 * x * (1.0 + jnp.tanh(0.7978845608028654 * (x + 0.044715 * (x * x * x))))


def _softplus(x):
    return jnp.maximum(x, 0.0) + jnp.log1p(jnp.exp(-jnp.abs(x)))


def _rms(x, g):
    ms = jnp.mean(x * x, axis=-1, keepdims=True)
    return x * lax.rsqrt(ms + NORM_EPS) * g


def _hilo(x):
    hi = x.astype(BF16)
    lo = (x - hi.astype(F32)).astype(BF16)
    return jnp.concatenate([hi, lo], axis=-1)


def _bdot(a, b):
    return jnp.dot(a.astype(BF16), b.astype(BF16), preferred_element_type=F32)


def _inproj_body(x_ref, g_ref, w_ref, o_ref, hn_ref):
    @pl.when(pl.program_id(1) == 0)
    def _():
        hn_ref[...] = _rms(x_ref[...], g_ref[...]).astype(BF16)

    o_ref[...] = jnp.dot(hn_ref[...], w_ref[...], preferred_element_type=F32)


def _inproj(x2d, g, w_bf16):
    n = x2d.shape[0]
    tm = min(512, n)
    tn = PCOLS // 3
    return pl.pallas_call(
        _inproj_body,
        grid=(n // tm, PCOLS // tn),
        in_specs=[
            pl.BlockSpec((tm, D_MODEL), lambda i, j: (i, 0)),
            pl.BlockSpec((1, D_MODEL), lambda i, j: (0, 0)),
            pl.BlockSpec((D_MODEL, tn), lambda i, j: (0, j)),
        ],
        out_specs=pl.BlockSpec((tm, tn), lambda i, j: (i, j)),
        out_shape=jax.ShapeDtypeStruct((n, PCOLS), F32),
        scratch_shapes=[pltpu.VMEM((tm, D_MODEL), BF16)],
        compiler_params=_cparams(("arbitrary", "arbitrary")),
        name="inproj",
    )(x2d, g, w_bf16)


def _s5_body(u_ref, bcat_ref, sc_ref, ccat_ref, d_ref, wglu_ref, bglu_ref, gain_ref, o_ref,
             hre_ref, him_ref, cre_ref, cim_ref):
    t = pl.program_id(1)
    tt = u_ref.shape[1]

    @pl.when(t == 0)
    def _():
        cre_ref[...] = jnp.zeros_like(cre_ref)
        cim_ref[...] = jnp.zeros_like(cim_ref)

    u = u_ref[0]
    bu = jnp.dot(u.astype(BF16), bcat_ref[...], preferred_element_type=F32)
    hre_ref[...] = bu[:, :S5_NST]
    him_ref[...] = bu[:, S5_NST:]

    def tile_step(j, carry):
        cr, ci = carry
        r0 = pl.multiple_of(j * SUBLANES, SUBLANES)
        ur = hre_ref[pl.ds(r0, SUBLANES), :]
        ui = him_ref[pl.ds(r0, SUBLANES), :]
        for k, d in enumerate((1, 2, 4)):
            ar = sc_ref[2 * k]
            ai = sc_ref[2 * k + 1]
            sr = pltpu.roll(ur, d, 0)
            si = pltpu.roll(ui, d, 0)
            ur, ui = ur + (ar * sr - ai * si), ui + (ar * si + ai * sr)
        pr = sc_ref[6]
        pi = sc_ref[7]
        hr = ur + (pr * cr - pi * ci)
        hi = ui + (pr * ci + pi * cr)
        hre_ref[pl.ds(r0, SUBLANES), :] = hr
        him_ref[pl.ds(r0, SUBLANES), :] = hi
        return hr[SUBLANES - 1:SUBLANES, :], hi[SUBLANES - 1:SUBLANES, :]

    cr, ci = lax.fori_loop(0, tt // SUBLANES, tile_step,
                           (cre_ref[SUBLANES - 1:SUBLANES, :], cim_ref[SUBLANES - 1:SUBLANES, :]))
    cre_ref[...] = jnp.broadcast_to(cr, cre_ref.shape)
    cim_ref[...] = jnp.broadcast_to(ci, cim_ref.shape)

    hcat = jnp.concatenate([hre_ref[...].astype(BF16), him_ref[...].astype(BF16)], axis=-1)
    y = jnp.dot(hcat, ccat_ref[...], preferred_element_type=F32) + d_ref[...] * u
    y = _gelu(y)
    y = y * _sigmoid(jnp.dot(y.astype(BF16), wglu_ref[...], preferred_element_type=F32) + bglu_ref[...])
    o_ref[0] = _rms(y, gain_ref[...])


def _s5_params(lam_re, lam_im, b_re, b_im, c_re, c_im, log_dt):
    dt = jnp.exp(log_dt)[:, None]
    mag = jnp.exp(lam_re * dt)
    abar_re, abar_im = mag * jnp.cos(lam_im * dt), mag * jnp.sin(lam_im * dt)
    den = lam_re * lam_re + lam_im * lam_im
    q_re = ((abar_re - 1.0) * lam_re + abar_im * lam_im) / den
    q_im = (abar_im * lam_re - (abar_re - 1.0) * lam_im) / den
    bbar_re = q_re[..., None] * b_re - q_im[..., None] * b_im
    bbar_im = q_re[..., None] * b_im + q_im[..., None] * b_re
    eye = jnp.eye(S5_GROUPS, dtype=F32)

    def in_blockdiag(m):
        return jnp.einsum('gnh,gk->ghkn', m, eye).reshape(S5_GROUPS * S5_CH, S5_NST)

    def out_blockdiag(m):
        return jnp.einsum('ghn,gk->gnkh', m, eye).reshape(S5_NST, S5_GROUPS * S5_CH)

    bcat = jnp.concatenate([in_blockdiag(bbar_re), in_blockdiag(bbar_im)], axis=1).astype(BF16)
    ccat = jnp.concatenate([out_blockdiag(c_re), -out_blockdiag(c_im)], axis=0).astype(BF16)

    ar = abar_re.reshape(1, S5_NST)
    ai = abar_im.reshape(1, S5_NST)
    pows = [(ar, ai)]
    for _ in range(SUBLANES - 1):
        pr, pi = pows[-1]
        pows.append((pr * ar - pi * ai, pr * ai + pi * ar))
    row = jnp.arange(SUBLANES)[:, None]
    tiles = []
    for d in (1, 2, 4):
        pr, pi = pows[d - 1]
        tiles.append(jnp.where(row >= d, pr, 0.0))
        tiles.append(jnp.where(row >= d, pi, 0.0))
    tiles.append(jnp.concatenate([p[0] for p in pows], axis=0))
    tiles.append(jnp.concatenate([p[1] for p in pows], axis=0))
    sc = jnp.stack(tiles, axis=0)
    return bcat, ccat, sc


def _s5(proj3, bcat, sc, ccat, d_skip, w_glu, b_glu, gain, tt):
    bsz, s, _ = proj3.shape
    const = lambda *shape: pl.BlockSpec(shape, lambda b, t: (0,) * len(shape))
    return pl.pallas_call(
        _s5_body,
        grid=(bsz, s // tt),
        in_specs=[
            pl.BlockSpec((1, tt, GROUP_W), lambda b, t: (b, t, COL_S5 // GROUP_W)),
            const(GROUP_W, 2 * S5_NST),
            const(8, SUBLANES, S5_NST),
            const(2 * S5_NST, GROUP_W),
            const(1, GROUP_W),
            const(GROUP_W, GROUP_W),
            const(1, GROUP_W),
            const(1, GROUP_W),
        ],
        out_specs=pl.BlockSpec((1, tt, GROUP_W), lambda b, t: (b, t, 0)),
        out_shape=jax.ShapeDtypeStruct((bsz, s, GROUP_W), F32),
        scratch_shapes=[pltpu.VMEM((tt, S5_NST), F32), pltpu.VMEM((tt, S5_NST), F32),
                        pltpu.VMEM((SUBLANES, S5_NST), F32), pltpu.VMEM((SUBLANES, S5_NST), F32)],
        compiler_params=_cparams(("arbitrary", "arbitrary")),
        name="s5",
    )(proj3, bcat, sc, ccat, d_skip, w_glu, b_glu, gain)


def _load_ext(ext_ref, x, first):
    tt = x.shape[0]

    @pl.when(first)
    def _():
        ext_ref[0:SUBLANES, :] = jnp.zeros((SUBLANES, x.shape[1]), F32)

    @pl.when(jnp.logical_not(first))
    def _():
        ext_ref[0:SUBLANES, :] = ext_ref[tt:tt + SUBLANES, :]

    ext_ref[SUBLANES:SUBLANES + tt, :] = x


def _causal_conv(ext_ref, cw_ref, cb_ref, tt):
    y = cb_ref[...] + cw_ref[CONV_K - 1:CONV_K, :] * ext_ref[SUBLANES:SUBLANES + tt, :]
    for i in range(CONV_K - 1):
        off = SUBLANES - (CONV_K - 1) + i
        y = y + cw_ref[i:i + 1, :] * ext_ref[off:off + tt, :]
    return y


def _lru_body(x_ref, gate_ref, cw_ref, cb_ref, wa_ref, ba_ref, wx_ref, bx_ref, ap_ref, gain_ref, o_ref,
              ext_ref, a_ref, u_ref, c_ref):
    t = pl.program_id(1)
    tt = x_ref.shape[1]

    @pl.when(t == 0)
    def _():
        c_ref[...] = jnp.zeros_like(c_ref)

    _load_ext(ext_ref, x_ref[0], t == 0)
    xc = _causal_conv(ext_ref, cw_ref, cb_ref, tt)
    xcb = xc.astype(BF16)
    r = _sigmoid(jnp.dot(xcb, wa_ref[...], preferred_element_type=F32) + ba_ref[...])
    i = _sigmoid(jnp.dot(xcb, wx_ref[...], preferred_element_type=F32) + bx_ref[...])
    log_a = -LRU_C * r * _softplus(-ap_ref[...])
    a_ref[...] = jnp.exp(log_a)
    th = jnp.tanh(log_a)
    u_ref[...] = jnp.sqrt(-2.0 * th / (1.0 - th)) * (i * xc)

    row = lax.broadcasted_iota(jnp.int32, (SUBLANES, GROUP_W), 0)

    def tile_step(j, c):
        r0 = pl.multiple_of(j * SUBLANES, SUBLANES)
        a = a_ref[pl.ds(r0, SUBLANES), :]
        u = u_ref[pl.ds(r0, SUBLANES), :]
        for d in (1, 2, 4):
            keep = row >= d
            a_sh = jnp.where(keep, pltpu.roll(a, d, 0), 1.0)
            u_sh = jnp.where(keep, pltpu.roll(u, d, 0), 0.0)
            u = u + a * u_sh
            a = a * a_sh
        h = u + a * c
        u_ref[pl.ds(r0, SUBLANES), :] = h
        return h[SUBLANES - 1:SUBLANES, :]

    c = lax.fori_loop(0, tt // SUBLANES, tile_step, c_ref[SUBLANES - 1:SUBLANES, :])
    c_ref[...] = jnp.broadcast_to(c, c_ref.shape)

    y = u_ref[...] * _gelu(gate_ref[0])
    o_ref[0] = _rms(y, gain_ref[...])


def _blockdiag(w):
    h, a, b = w.shape
    return jnp.einsum('hij,hk->hikj', w, jnp.eye(h, dtype=w.dtype)).reshape(h * a, h * b)


def _lru(proj3, cw, cb, wa_bd, ba, wx_bd, bx, a_param, gain, tt):
    bsz, s, _ = proj3.shape
    const = lambda *shape: pl.BlockSpec(shape, lambda b, t: (0,) * len(shape))
    return pl.pallas_call(
        _lru_body,
        grid=(bsz, s // tt),
        in_specs=[
            pl.BlockSpec((1, tt, GROUP_W), lambda b, t: (b, t, COL_LX // GROUP_W)),
            pl.BlockSpec((1, tt, GROUP_W), lambda b, t: (b, t, COL_LG // GROUP_W)),
            const(CONV_K, GROUP_W), const(1, GROUP_W),
            const(GROUP_W, GROUP_W), const(1, GROUP_W),
            const(GROUP_W, GROUP_W), const(1, GROUP_W),
            const(1, GROUP_W), const(1, GROUP_W),
        ],
        out_specs=pl.BlockSpec((1, tt, GROUP_W), lambda b, t: (b, t, 0)),
        out_shape=jax.ShapeDtypeStruct((bsz, s, GROUP_W), F32),
        scratch_shapes=[pltpu.VMEM((tt + SUBLANES, GROUP_W), F32), pltpu.VMEM((tt, GROUP_W), F32),
                        pltpu.VMEM((tt, GROUP_W), F32), pltpu.VMEM((SUBLANES, GROUP_W), F32)],
        compiler_params=_cparams(("arbitrary", "arbitrary")),
        name="rglru",
    )(proj3, proj3, cw, cb, wa_bd, ba, wx_bd, bx, a_param, gain)


def _pair_order(x):
    return jnp.concatenate([x[:, h * RW_HEAD_DIM:(h + 1) * RW_HEAD_DIM] for h in RW_PAIR_ORDER], axis=-1)


def _rwpre_body(r_ref, k_ref, v_ref, l_ref, mu_r, mu_k, mu_v, mu_l, w0_ref, w2_ref, a0_ref, a2_ref, g2_ref,
                kk_ref, ka_ref, rk_ref, hm_ref,
                or_ref, ow_ref, ok_ref, ov_ref, okk_ref, onb_ref, og_ref, obon_ref,
                er_ref, ek_ref, ev_ref, el_ref):
    t = pl.program_id(1)
    tt = r_ref.shape[1]
    first = t == 0

    def lerp(ext_ref, x_ref, mu_ref):
        x = x_ref[0]
        _load_ext(ext_ref, x, first)
        prev = ext_ref[SUBLANES - 1:SUBLANES - 1 + tt, :]
        return x + (prev - x) * mu_ref[...]

    r = lerp(er_ref, r_ref, mu_r)
    k = lerp(ek_ref, k_ref, mu_k)
    v = lerp(ev_ref, v_ref, mu_v)
    lo = lerp(el_ref, l_ref, mu_l)

    w_log = -_softplus(-(w0_ref[...] + _bdot(jnp.tanh(lo), w2_ref[...]))) - 0.5
    decay = jnp.exp(-jnp.exp(w_log))
    a = _sigmoid(a0_ref[...] + _bdot(lo, a2_ref[...]))
    g = _bdot(_sigmoid(lo), g2_ref[...])
    kk = k * kk_ref[...]
    ss = jnp.dot(_hilo(kk * kk), hm_ref[...], preferred_element_type=F32)
    kk = kk * lax.rsqrt(jnp.maximum(ss, 1e-24))
    k2 = k * (1.0 + (a - 1.0) * ka_ref[...])
    srk = jnp.dot(_hilo(r * k2 * rk_ref[...]), hm_ref[...], preferred_element_type=F32)

    or_ref[0] = _pair_order(r)
    ow_ref[0] = _pair_order(decay)
    ok_ref[0] = _pair_order(k2)
    ov_ref[0] = _pair_order(v)
    okk_ref[0] = _pair_order(kk)
    onb_ref[0] = _pair_order(-(kk * a))
    og_ref[0] = g
    obon_ref[0] = srk * v


def _rwpre(proj3, mu_r, mu_k, mu_v, mu_l, w0, w2p, a0, a2p, g2p, k_k, k_a, r_k, hm2, tt):
    bsz, s, _ = proj3.shape
    const = lambda *shape: pl.BlockSpec(shape, lambda b, t: (0,) * len(shape))
    col = lambda c, w: pl.BlockSpec((1, tt, w), lambda b, t: (b, t, c // w))
    out = pl.BlockSpec((1, tt, GROUP_W), lambda b, t: (b, t, 0))
    sds = jax.ShapeDtypeStruct((bsz, s, GROUP_W), F32)
    return pl.pallas_call(
        _rwpre_body,
        grid=(bsz, s // tt),
        in_specs=[
            col(COL_R, GROUP_W), col(COL_K, GROUP_W), col(COL_V, GROUP_W), col(COL_LORA, RW_LORA_PAD),
            const(1, GROUP_W), const(1, GROUP_W), const(1, GROUP_W), const(1, RW_LORA_PAD),
            const(1, GROUP_W), const(RW_LORA_PAD, GROUP_W),
            const(1, GROUP_W), const(RW_LORA_PAD, GROUP_W), const(RW_LORA_PAD, GROUP_W),
            const(1, GROUP_W), const(1, GROUP_W), const(1, GROUP_W),
            const(2 * GROUP_W, GROUP_W),
        ],
        out_specs=[out] * 8,
        out_shape=[sds] * 8,
        scratch_shapes=[pltpu.VMEM((tt + SUBLANES, GROUP_W), F32)] * 3
                       + [pltpu.VMEM((tt + SUBLANES, RW_LORA_PAD), F32)],
        compiler_params=_cparams(("arbitrary", "arbitrary")),
        name="rwkv_pre",
    )(proj3, proj3, proj3, proj3, mu_r, mu_k, mu_v, mu_l, w0, w2p, a0, a2p, g2p, k_k, k_a, r_k, hm2)


def _rwscan_body(r_ref, w_ref, k_ref, v_ref, kk_ref, nb_ref, hmhm_ref, gg_ref, d2_ref, o_ref, s_ref):
    bsz = r_ref.shape[0]
    ngroups = r_ref.shape[1]
    npair = GROUP_W // LANES
    hd = RW_HEAD_DIM

    @pl.when(pl.program_id(0) == 0)
    def _():
        s_ref[...] = jnp.zeros_like(s_ref)

    hmhm = hmhm_ref[...]
    gg = gg_ref[...]
    d2 = d2_ref[...]
    rowid = lax.broadcasted_iota(jnp.int32, (SUBLANES, GROUP_W), 0)

    def group(g, carry):
        tiles = [[ref[b, g] for b in range(bsz)] for ref in (r_ref, w_ref, k_ref, v_ref, kk_ref, nb_ref)]
        state = s_ref[...]
        out_tiles = [jnp.zeros((SUBLANES, GROUP_W), F32) for _ in range(bsz)]
        for i in range(SUBLANES):
            def rows(which):
                return jnp.concatenate(
                    [jnp.broadcast_to(tiles[which][b][i:i + 1, q * LANES:(q + 1) * LANES], (hd, LANES))
                     for b in range(bsz) for q in range(npair)], axis=0)
            r_b, w_b, k_b, v_b, kk_b, nb_b = (rows(n) for n in range(6))
            sa = jnp.dot(_hilo(state * kk_b), hmhm, preferred_element_type=F32)
            ev = jnp.dot(_hilo(d2 * v_b), hmhm, preferred_element_type=F32)
            state = state * w_b + sa * nb_b + ev * k_b
            o2 = lax.dot_general(gg, _hilo(state * r_b), (((1,), (1,)), ((), ())),
                                 preferred_element_type=F32)
            half = npair * hd
            for b in range(bsz):
                orow = jnp.concatenate([o2[0:1, b * half:(b + 1) * half], o2[1:2, b * half:(b + 1) * half]], axis=1)
                out_tiles[b] = jnp.where(rowid == i, jnp.broadcast_to(orow, (SUBLANES, GROUP_W)), out_tiles[b])
        s_ref[...] = state
        for b in range(bsz):
            o_ref[b, g] = out_tiles[b]
        return carry

    lax.fori_loop(0, ngroups, group, 0)


def _rwscan(seqs, hmhm, gg, d2all, tt):
    bsz, s, _ = seqs[0].shape
    view = lambda x: x.reshape(bsz, s // SUBLANES, SUBLANES, GROUP_W)
    blk = pl.BlockSpec((bsz, tt // SUBLANES, SUBLANES, GROUP_W), lambda t: (0, t, 0, 0))
    const = lambda *shape: pl.BlockSpec(shape, lambda t: (0,) * len(shape))
    nrow = bsz * (GROUP_W // LANES) * RW_HEAD_DIM
    out = pl.pallas_call(
        _rwscan_body,
        grid=(s // tt,),
        in_specs=[blk] * 6 + [const(2 * LANES, LANES), const(SUBLANES, 2 * LANES), const(nrow, LANES)],
        out_specs=blk,
        out_shape=jax.ShapeDtypeStruct((bsz, s // SUBLANES, SUBLANES, GROUP_W), F32),
        scratch_shapes=[pltpu.VMEM((nrow, LANES), F32)],
        compiler_params=_cparams(("arbitrary",)),
        name="rwkv_scan",
    )(*[view(x) for x in seqs], hmhm, gg, d2all)
    return out.reshape(bsz, s, GROUP_W)


def _rwpost_body(o_ref, bon_ref, g_ref, lnw_ref, lnb_ref, hm_ref, y_ref):
    o = o_ref[0]
    inv = 1.0 / RW_HEAD_DIM
    mean = jnp.dot(_hilo(o), hm_ref[...], preferred_element_type=F32) * inv
    c = o - mean
    var = jnp.dot(_hilo(c * c), hm_ref[...], preferred_element_type=F32) * inv
    on = c * lax.rsqrt(var + RW_LN_EPS) * lnw_ref[...] + lnb_ref[...]
    y_ref[0] = (on + bon_ref[0]) * g_ref[0]


def _rwpost(o, bonus, g, ln_w, ln_b, hm2, tt):
    bsz, s, _ = o.shape
    blk = pl.BlockSpec((1, tt, GROUP_W), lambda b, t: (b, t, 0))
    const = lambda *shape: pl.BlockSpec(shape, lambda b, t: (0,) * len(shape))
    return pl.pallas_call(
        _rwpost_body,
        grid=(bsz, s // tt),
        in_specs=[blk, blk, blk, const(1, GROUP_W), const(1, GROUP_W), const(2 * GROUP_W, GROUP_W)],
        out_specs=blk,
        out_shape=jax.ShapeDtypeStruct((bsz, s, GROUP_W), F32),
        compiler_params=_cparams(("arbitrary", "arbitrary")),
        name="rwkv_post",
    )(o, bonus, g, ln_w, ln_b, hm2)


def _ssd_body(z_ref, xbc_ref, dt_ref, cw_ref, cb_ref, dtb_ref, acoef_ref, eh_ref, dskip_ref, nw_ref, tril_ref,
              o_ref, ext_ref, ht_ref, yd_ref):
    t = pl.program_id(1)
    tt = SSD_CHUNK
    gw = SSD_STATE
    hpg = SSD_HEADS // SSD_GROUPS
    gl = hpg * SSD_HEAD_DIM

    @pl.when(t == 0)
    def _():
        ht_ref[...] = jnp.zeros_like(ht_ref)

    _load_ext(ext_ref, xbc_ref[0], t == 0)
    xbc = _silu(_causal_conv(ext_ref, cw_ref, cb_ref, tt))
    xs = xbc[:, :GROUP_W]
    bm = xbc[:, GROUP_W:GROUP_W + SSD_GROUPS * gw]
    cm = xbc[:, GROUP_W + SSD_GROUPS * gw:]

    dt = _softplus(dt_ref[0] + dtb_ref[...])
    a = dt * acoef_ref[...]
    acs = jnp.dot(tril_ref[...], a, preferred_element_type=F32, precision=HIGHEST)
    acs_t = jnp.transpose(acs)
    acs_e = jnp.dot(acs, eh_ref[...], preferred_element_type=F32, precision=HIGHEST)
    dt_e = jnp.dot(dt, eh_ref[...], preferred_element_type=F32, precision=HIGHEST)
    xdt = xs * dt_e
    last = acs_e[tt - 1:tt, :]
    xdtw = xdt * jnp.exp(last - acs_e)
    causal = (lax.broadcasted_iota(jnp.int32, (tt, tt), 0) >= lax.broadcasted_iota(jnp.int32, (tt, tt), 1))

    y_off = []
    for g in range(SSD_GROUPS):
        bg = bm[:, g * gw:(g + 1) * gw]
        cg = cm[:, g * gw:(g + 1) * gw]
        scores = lax.dot_general(cg.astype(BF16), bg.astype(BF16), (((1,), (1,)), ((), ())),
                                 preferred_element_type=F32)
        for hh in range(hpg):
            h = g * hpg + hh
            seg = acs[:, h:h + 1] - acs_t[h:h + 1, :]
            dec = jnp.where(causal, jnp.exp(jnp.where(causal, seg, 0.0)), 0.0)
            yd_ref[:, h * SSD_HEAD_DIM:(h + 1) * SSD_HEAD_DIM] = _bdot(
                scores * dec, xdt[:, h * SSD_HEAD_DIM:(h + 1) * SSD_HEAD_DIM])
        h_in = ht_ref[g]
        y_off.append(_bdot(cg, h_in))
        st = _bdot(jnp.transpose(bg), xdtw[:, g * gl:(g + 1) * gl])
        ht_ref[g] = h_in * jnp.exp(last[:, g * gl:(g + 1) * gl]) + st

    y = yd_ref[...] + jnp.concatenate(y_off, axis=-1) * jnp.exp(acs_e) + xs * dskip_ref[...]
    y = y * _silu(z_ref[0])
    o_ref[0] = _rms(y, nw_ref[...])


def _ssd(proj3, cw, cb, dtb, acoef, eh, dskip_e, norm_w, tril):
    bsz, s, _ = proj3.shape
    tt = SSD_CHUNK
    const = lambda *shape: pl.BlockSpec(shape, lambda b, t: (0,) * len(shape))
    return pl.pallas_call(
        _ssd_body,
        grid=(bsz, s // tt),
        in_specs=[
            pl.BlockSpec((1, tt, GROUP_W), lambda b, t: (b, t, COL_Z // GROUP_W)),
            pl.BlockSpec((1, tt, SSD_XBC), lambda b, t: (b, t, COL_XBC // SSD_XBC)),
            pl.BlockSpec((1, tt, LANES), lambda b, t: (b, t, COL_DT // LANES)),
            const(CONV_K, SSD_XBC), const(1, SSD_XBC), const(1, LANES), const(1, LANES),
            const(LANES, GROUP_W), const(1, GROUP_W), const(1, GROUP_W), const(tt, tt),
        ],
        out_specs=pl.BlockSpec((1, tt, GROUP_W), lambda b, t: (b, t, 0)),
        out_shape=jax.ShapeDtypeStruct((bsz, s, GROUP_W), F32),
        scratch_shapes=[pltpu.VMEM((tt + SUBLANES, SSD_XBC), F32),
                        pltpu.VMEM((SSD_GROUPS, SSD_STATE, GROUP_W // SSD_GROUPS), F32),
                        pltpu.VMEM((tt, GROUP_W), F32)],
        compiler_params=_cparams(("arbitrary", "arbitrary")),
        name="ssd",
    )(proj3, proj3, proj3, cw, cb, dtb, acoef, eh, dskip_e, norm_w, tril)


def _outproj_body(y0_ref, y1_ref, y2_ref, y3_ref, x_ref, wo_ref, g_ref, wr_ref, br_ref,
                  x1_ref, h_ref, eid_ref, gate_ref):
    acc = x_ref[...]
    for k, y_ref in enumerate((y0_ref, y1_ref, y2_ref, y3_ref)):
        acc = acc + jnp.dot(y_ref[...].astype(BF16), wo_ref[k * GROUP_W:(k + 1) * GROUP_W, :],
                            preferred_element_type=F32)
    x1_ref[...] = acc
    h = _rms(acc, g_ref[...])
    h_ref[...] = h

    logits = jnp.dot(h, wr_ref[...], preferred_element_type=F32, precision=HIGHEST) + br_ref[...]
    lane = lax.broadcasted_iota(jnp.int32, logits.shape, 1)
    ninf = -jnp.inf
    big = jnp.int32(LANES)
    gl = jnp.where(lane < MOE_GROUPS, logits, ninf)
    gmax = jnp.max(gl, axis=-1, keepdims=True)
    gidx = jnp.min(jnp.where(gl == gmax, lane, big), axis=-1, keepdims=True)
    gprob = 1.0 / jnp.sum(jnp.exp(gl - gmax), axis=-1, keepdims=True)
    eidx = lane - MOE_GROUPS
    in_group = (eidx >= 0) & (eidx < MOE_EXPERTS) & ((eidx >> 3) == gidx)
    el = jnp.where(in_group, logits, ninf)
    v1 = jnp.max(el, axis=-1, keepdims=True)
    i1 = jnp.min(jnp.where(el == v1, lane, big), axis=-1, keepdims=True)
    el2 = jnp.where(lane == i1, ninf, el)
    v2 = jnp.max(el2, axis=-1, keepdims=True)
    i2 = jnp.min(jnp.where(el2 == v2, lane, big), axis=-1, keepdims=True)
    e2 = jnp.exp(v2 - v1)
    p1 = 1.0 / (1.0 + e2)
    p2 = e2 * p1
    eid_ref[...] = jnp.where(lane == 0, i1 - MOE_GROUPS, jnp.where(lane == 1, i2 - MOE_GROUPS, 0))
    gate_ref[...] = jnp.where(lane == 0, gprob * p1, jnp.where(lane == 1, gprob * p2, 0.0))


def _outproj(ys, x2d, wo_bf16, g, wr, br):
    n = x2d.shape[0]
    tm = min(256, n)
    yblk = pl.BlockSpec((tm, GROUP_W), lambda i: (i, 0))
    xblk = pl.BlockSpec((tm, D_MODEL), lambda i: (i, 0))
    rblk = pl.BlockSpec((tm, LANES), lambda i: (i, 0))
    const = lambda *shape: pl.BlockSpec(shape, lambda i: (0,) * len(shape))
    return pl.pallas_call(
        _outproj_body,
        grid=(n // tm,),
        in_specs=[yblk] * 4 + [xblk, const(D_MODEL, D_MODEL), const(1, D_MODEL), const(D_MODEL, LANES),
                               const(1, LANES)],
        out_specs=[xblk, xblk, rblk, rblk],
        out_shape=[jax.ShapeDtypeStruct((n, D_MODEL), F32), jax.ShapeDtypeStruct((n, D_MODEL), F32),
                   jax.ShapeDtypeStruct((n, LANES), jnp.int32), jax.ShapeDtypeStruct((n, LANES), F32)],
        compiler_params=_cparams(("arbitrary",)),
        name="outproj_router",
    )(*ys, x2d, wo_bf16, g, wr, br)


def _moe_body(blk_e_ref, nused_ref, src_ref, dst_ref,
              h_hbm, gate_ref, w1_ref, w3_ref, w2_ref, y_hbm,
              xbuf, ybuf, gsem, ssem):
    i = pl.program_id(0)
    nused = nused_ref[0]
    slot = i % 2

    def gather_copy(blk, sl, r):
        tok = src_ref[blk * MOE_BLOCK + r]
        return pltpu.make_async_copy(h_hbm.at[pl.ds(tok, 1), :], xbuf.at[sl, pl.ds(r, 1), :], gsem.at[sl])

    def scatter_copy(blk, sl, r):
        row = dst_ref[blk * MOE_BLOCK + r]
        return pltpu.make_async_copy(ybuf.at[sl, pl.ds(r, 1), :], y_hbm.at[pl.ds(row, 1), :], ssem.at[sl])

    def start_gather(blk, sl):
        def body(r, c):
            gather_copy(blk, sl, r).start()
            return c
        lax.fori_loop(0, MOE_BLOCK, body, 0)

    def wait_gather(blk, sl):
        def body(r, c):
            gather_copy(blk, sl, r).wait()
            return c
        lax.fori_loop(0, MOE_BLOCK, body, 0)

    def start_scatter(blk, sl):
        def body(r, c):
            scatter_copy(blk, sl, r).start()
            return c
        lax.fori_loop(0, MOE_BLOCK, body, 0)

    def wait_scatter(blk, sl):
        def body(r, c):
            scatter_copy(blk, sl, r).wait()
            return c
        lax.fori_loop(0, MOE_BLOCK, body, 0)

    @pl.when(i < nused)
    def _():
        @pl.when(i == 0)
        def _():
            start_gather(i, slot)

        @pl.when(i + 1 < nused)
        def _():
            start_gather(i + 1, 1 - slot)

        wait_gather(i, slot)
        xb = xbuf[slot].astype(BF16)
        h1 = jnp.dot(xb, w1_ref[0].astype(BF16), preferred_element_type=F32)
        h3 = jnp.dot(xb, w3_ref[0].astype(BF16), preferred_element_type=F32)
        act = (_silu(h1) * h3).astype(BF16)
        y = jnp.dot(act, w2_ref[0].astype(BF16), preferred_element_type=F32) * gate_ref[0]

        @pl.when(i >= 2)
        def _():
            wait_scatter(i - 2, slot)

        ybuf[slot] = y
        start_scatter(i, slot)

        @pl.when(i == nused - 1)
        def _():
            @pl.when(i >= 1)
            def _():
                wait_scatter(i - 1, 1 - slot)
            wait_scatter(i, slot)


def _moe_ffn(blk_e, nused, src_tok, dst_row, h2d, gate_pad, w1, w3, w2, nrows_out):
    nb = blk_e.shape[0]
    grid_spec = pltpu.PrefetchScalarGridSpec(
        num_scalar_prefetch=4,
        grid=(nb,),
        in_specs=[
            pl.BlockSpec(memory_space=pl.ANY),
            pl.BlockSpec((1, MOE_BLOCK, 1), lambda i, be, nu, s, d: (i, 0, 0)),
            pl.BlockSpec((1, D_MODEL, MOE_DFF), lambda i, be, nu, s, d: (be[i], 0, 0)),
            pl.BlockSpec((1, D_MODEL, MOE_DFF), lambda i, be, nu, s, d: (be[i], 0, 0)),
            pl.BlockSpec((1, MOE_DFF, D_MODEL), lambda i, be, nu, s, d: (be[i], 0, 0)),
        ],
        out_specs=pl.BlockSpec(memory_space=pl.ANY),
        scratch_shapes=[pltpu.VMEM((2, MOE_BLOCK, D_MODEL), F32), pltpu.VMEM((2, MOE_BLOCK, D_MODEL), F32),
                        pltpu.SemaphoreType.DMA((2,)), pltpu.SemaphoreType.DMA((2,))],
    )
    return pl.pallas_call(
        _moe_body,
        grid_spec=grid_spec,
        out_shape=jax.ShapeDtypeStruct((nrows_out, D_MODEL), F32),
        compiler_params=_cparams(("arbitrary",)),
        name="moe_ffn",
    )(blk_e, nused, src_tok, dst_row, h2d, gate_pad, w1, w3, w2)


def _dispatch(eid, gate, n):
    na = n * MOE_TOPK
    eflat = eid.reshape(na)
    order = jnp.argsort(eflat)
    eid_s = eflat[order]
    counts = jnp.bincount(eflat, length=MOE_EXPERTS)
    starts = jnp.cumsum(counts) - counts
    padded = (counts + MOE_BLOCK - 1) // MOE_BLOCK * MOE_BLOCK
    pends = jnp.cumsum(padded)
    pstarts = pends - padded
    dest = pstarts[eid_s] + (jnp.arange(na) - starts[eid_s])
    cap = -(-(na + MOE_EXPERTS * (MOE_BLOCK - 1)) // MOE_BLOCK) * MOE_BLOCK
    nb = cap // MOE_BLOCK
    npad = n + 2 * MOE_BLOCK
    pos = jnp.arange(cap)
    dump = n + ((pos // MOE_BLOCK) % 2) * MOE_BLOCK + pos % MOE_BLOCK
    src_tok = jnp.zeros((cap,), jnp.int32).at[dest].set((order // MOE_TOPK).astype(jnp.int32))
    dst_row = dump.astype(jnp.int32).at[dest].set(((order % MOE_TOPK) * npad + order // MOE_TOPK).astype(jnp.int32))
    gate_pad = jnp.zeros((cap,), F32).at[dest].set(gate.reshape(na)[order])
    blk_e = jnp.minimum(jnp.searchsorted(pends, jnp.arange(nb) * MOE_BLOCK, side='right'),
                        MOE_EXPERTS - 1).astype(jnp.int32)
    nused = (pends[-1] // MOE_BLOCK).astype(jnp.int32).reshape(1)
    return blk_e, nused, src_tok, dst_row, gate_pad.reshape(nb, MOE_BLOCK, 1), npad


def _combine_body(x_ref, y_ref, o_ref):
    o_ref[...] = x_ref[...] + y_ref[0] + y_ref[1]


def _combine_norm_body(x_ref, y_ref, g_ref, o_ref):
    o_ref[...] = _rms(x_ref[...] + y_ref[0] + y_ref[1], g_ref[...])


def _combine(x2d, y3, g=None):
    n = x2d.shape[0]
    tm = min(256, n)
    xblk = pl.BlockSpec((tm, D_MODEL), lambda i: (i, 0))
    yblk = pl.BlockSpec((2, tm, D_MODEL), lambda i: (0, i, 0))
    if g is None:
        body, ins, args = _combine_body, [xblk, yblk], (x2d, y3)
    else:
        body, ins, args = _combine_norm_body, [xblk, yblk, pl.BlockSpec((1, D_MODEL), lambda i: (0, 0))], (x2d, y3, g)
    return pl.pallas_call(
        body, grid=(n // tm,), in_specs=ins, out_specs=xblk,
        out_shape=jax.ShapeDtypeStruct((n, D_MODEL), F32),
        compiler_params=_cparams(("arbitrary",)),
        name="combine",
    )(*args)


def _pad_cols(w, width):
    return jnp.pad(w, ((0, 0), (0, width - w.shape[1])))


def _layout_w_in(w):
    s5 = w[:, 0:512]
    lx = w[:, 512:1024]
    lg = w[:, 1024:1536]
    r = w[:, 1536:2048]
    k = w[:, 2048:2560]
    v = w[:, 2560:3072]
    lora = _pad_cols(w[:, 3072:3232], RW_LORA_PAD)
    z = w[:, 3232:3744]
    xbc = w[:, 3744:4768]
    dt = _pad_cols(w[:, 4768:4776], LANES)
    return jnp.concatenate([xbc, s5, lx, lg, z, r, k, v, lora, dt], axis=1).astype(BF16)


def _head_ones(width, head):
    idx = np.arange(width) // head
    return (idx[:, None] == idx[None, :]).astype(np.float32)


def kernel(x, norm_mix, w_in, w_out, s5_lambda_re, s5_lambda_im, s5_b_re, s5_b_im, s5_c_re, s5_c_im, s5_d, s5_log_dt, s5_w_glu, s5_b_glu, s5_gain, lru_conv_w, lru_conv_b, lru_wa, lru_ba, lru_wx, lru_bx, lru_a_param, lru_gain, rw_mu, rw_w0, rw_w2, rw_a0, rw_a2, rw_g2, rw_k_k, rw_k_a, rw_r_k, rw_ln_w, rw_ln_b, ssd_conv_w, ssd_conv_b, ssd_dt_bias, ssd_a_log, ssd_d, ssd_norm_w, norm_ffn, moe_w_group, moe_b_group, moe_w_expert, moe_b_expert, moe_w1, moe_w3, moe_w2, norm_final):
    bsz, s, d = x.shape
    n = bsz * s
    tt = min(256, s)
    row = lambda v: v.reshape(1, -1).astype(F32)

    hm512 = _head_ones(GROUP_W, RW_HEAD_DIM)
    hm2 = jnp.asarray(np.concatenate([hm512, hm512], axis=0), BF16)
    hm128 = _head_ones(LANES, RW_HEAD_DIM)
    hmhm = jnp.asarray(np.concatenate([hm128, hm128], axis=0), BF16)
    gg_np = np.zeros((SUBLANES, 2 * LANES), np.float32)
    for j in range(2):
        gg_np[j, j * RW_HEAD_DIM:(j + 1) * RW_HEAD_DIM] = 1.0
        gg_np[j, LANES + j * RW_HEAD_DIM:LANES + (j + 1) * RW_HEAD_DIM] = 1.0
    gg = jnp.asarray(gg_np, BF16)
    vv = np.arange(RW_HEAD_DIM)
    d2_np = (vv[:, None] == (np.arange(LANES) % RW_HEAD_DIM)[None, :]).astype(np.float32)
    d2all = jnp.asarray(np.tile(d2_np, (bsz * (GROUP_W // LANES), 1)), F32)
    eh_np = np.zeros((LANES, GROUP_W), np.float32)
    for h in range(SSD_HEADS):
        eh_np[h, h * SSD_HEAD_DIM:(h + 1) * SSD_HEAD_DIM] = 1.0
    eh = jnp.asarray(eh_np, F32)
    tril = jnp.asarray(np.tril(np.ones((SSD_CHUNK, SSD_CHUNK), np.float32)), F32)

    x2d = x.reshape(n, d)
    out = None
    for l in range(DEPTH):
        proj = _inproj(x2d, row(norm_mix[l]), _layout_w_in(w_in[l]))
        proj3 = proj.reshape(bsz, s, PCOLS)

        bcat, ccat, sc = _s5_params(s5_lambda_re[l], s5_lambda_im[l], s5_b_re[l], s5_b_im[l],
                                    s5_c_re[l], s5_c_im[l], s5_log_dt[l])
        y_s5 = _s5(proj3, bcat, sc, ccat, row(s5_d[l]), s5_w_glu[l].astype(BF16), row(s5_b_glu[l]),
                   row(s5_gain[l]), tt)

        y_lru = _lru(proj3, lru_conv_w[l], row(lru_conv_b[l]), _blockdiag(lru_wa[l]).astype(BF16), row(lru_ba[l]),
                     _blockdiag(lru_wx[l]).astype(BF16), row(lru_bx[l]), row(lru_a_param[l]), row(lru_gain[l]), tt)

        mu = rw_mu[l]
        mu_l = jnp.pad(mu[3 * GROUP_W:], (0, RW_LORA_PAD - (RW_DECAY_LORA + RW_AAA_LORA + RW_GATE_LORA)))
        w2p = jnp.zeros((RW_LORA_PAD, GROUP_W), F32).at[0:RW_DECAY_LORA].set(rw_w2[l]).astype(BF16)
        a2p = jnp.zeros((RW_LORA_PAD, GROUP_W), F32).at[RW_DECAY_LORA:RW_DECAY_LORA + RW_AAA_LORA].set(
            rw_a2[l]).astype(BF16)
        g_lo = RW_DECAY_LORA + RW_AAA_LORA
        g2p = jnp.zeros((RW_LORA_PAD, GROUP_W), F32).at[g_lo:g_lo + RW_GATE_LORA].set(rw_g2[l]).astype(BF16)
        seqs = _rwpre(proj3, row(mu[0:GROUP_W]), row(mu[GROUP_W:2 * GROUP_W]), row(mu[2 * GROUP_W:3 * GROUP_W]),
                      row(mu_l), row(rw_w0[l]), w2p, row(rw_a0[l]), a2p, g2p, row(rw_k_k[l]), row(rw_k_a[l]),
                      row(rw_r_k[l]), hm2, tt)
        o_rw = _rwscan(seqs[:6], hmhm, gg, d2all, tt)
        y_rw = _rwpost(o_rw, seqs[7], seqs[6], row(rw_ln_w[l]), row(rw_ln_b[l]), hm2, tt)

        dtb = jnp.pad(ssd_dt_bias[l], (0, LANES - SSD_HEADS)).reshape(1, LANES)
        acoef = jnp.pad(-jnp.exp(ssd_a_log[l]), (0, LANES - SSD_HEADS)).reshape(1, LANES)
        dskip_e = jnp.repeat(ssd_d[l], SSD_HEAD_DIM).reshape(1, GROUP_W)
        y_ssd = _ssd(proj3, ssd_conv_w[l], row(ssd_conv_b[l]), dtb, acoef, eh, dskip_e, row(ssd_norm_w[l]), tril)

        wr = _pad_cols(jnp.concatenate([moe_w_group[l], moe_w_expert[l]], axis=1), LANES)
        br = _pad_cols(jnp.concatenate([moe_b_group[l], moe_b_expert[l]]).reshape(1, -1), LANES)
        flat = lambda y: y.reshape(n, GROUP_W)
        x1, h2, eid, gate = _outproj([flat(y_s5), flat(y_lru), flat(y_rw), flat(y_ssd)], x2d,
                                     w_out[l].astype(BF16), row(norm_ffn[l]), wr, br)

        blk_e, nused, src_tok, dst_row, gate_pad, npad = _dispatch(eid[:, :MOE_TOPK], gate[:, :MOE_TOPK], n)
        y = _moe_ffn(blk_e, nused, src_tok, dst_row, h2, gate_pad, moe_w1[l], moe_w3[l], moe_w2[l],
                     MOE_TOPK * npad)
        y3 = y.reshape(MOE_TOPK, npad, d)
        if l + 1 < DEPTH:
            x2d = _combine(x1, y3)
        else:
            out = _combine(x1, y3, row(norm_final))
    return out.reshape(bsz, s, d)
```

```python
import functools

import numpy as np
import jax
import jax.numpy as jnp
from jax import lax
from jax.experimental import pallas as pl
from jax.experimental.pallas import tpu as pltpu

F32 = jnp.float32
BF16 = jnp.bfloat16
HIGHEST = lax.Precision.HIGHEST

D_MODEL = 2048
DEPTH = 4
GROUP_W = 512
NORM_EPS = 1e-6

S5_CH = 16
S5_GROUPS = 32
S5_STATE = 64
S5_NST = S5_GROUPS * S5_STATE

LRU_HEADS = 8
LRU_HEAD_DIM = 64
LRU_C = 8.0
CONV_K = 4

RW_HEADS = 8
RW_HEAD_DIM = 64
RW_DECAY_LORA = 32
RW_AAA_LORA = 32
RW_GATE_LORA = 96
RW_LORA_PAD = 256
RW_LN_EPS = 64e-5

SSD_HEADS = 8
SSD_HEAD_DIM = 64
SSD_GROUPS = 2
SSD_STATE = 128
SSD_CHUNK = 128
SSD_XBC = 1024

MOE_GROUPS = 8
MOE_PER_GROUP = 8
MOE_EXPERTS = 64
MOE_TOPK = 2
MOE_DFF = 384
MOE_BLOCK = 128

SUBLANES = 8
LANES = 128
VMEM_LIMIT = 56 * 1024 * 1024

COL_XBC = 0
COL_S5 = 1024
COL_LX = 1536
COL_LG = 2048
COL_Z = 2560
COL_R = 3072
COL_K = 3584
COL_V = 4096
COL_LORA = 4608
COL_DT = 4864
PCOLS = 4992

RW_PAIR_ORDER = (0, 4, 1, 5, 2, 6, 3, 7)


def _cparams(sem):
    return pltpu.CompilerParams(dimension_semantics=sem, vmem_limit_bytes=VMEM_LIMIT)


def _sigmoid(x):
    return 1.0 / (1.0 + jnp.exp(-x))


def _silu(x):
    return x * _sigmoid(x)


def _gelu(x):
    return 0.5 * x * (1.0 + jnp.tanh(0.7978845608028654 * (x + 0.044715 * (x * x * x))))


def _softplus(x):
    return jnp.maximum(x, 0.0) + jnp.log1p(jnp.exp(-jnp.abs(x)))


def _rms(x, g):
    ms = jnp.mean(x * x, axis=-1, keepdims=True)
    return x * lax.rsqrt(ms + NORM_EPS) * g


def _hilo(x):
    hi = x.astype(BF16)
    lo = (x - hi.astype(F32)).astype(BF16)
    return jnp.concatenate([hi, lo], axis=-1)


def _bdot(a, b):
    return jnp.dot(a.astype(BF16), b.astype(BF16), preferred_element_type=F32)


def _inproj_body(x_ref, g_ref, w_ref, o_ref, hn_ref):
    @pl.when(pl.program_id(1) == 0)
    def _():
        hn_ref[...] = _rms(x_ref[...], g_ref[...]).astype(BF16)

    o_ref[...] = jnp.dot(hn_ref[...], w_ref[...], preferred_element_type=F32)


def _inproj(x2d, g, w_bf16):
    n = x2d.shape[0]
    tm = min(512, n)
    tn = PCOLS // 3
    return pl.pallas_call(
        _inproj_body,
        grid=(n // tm, PCOLS // tn),
        in_specs=[
            pl.BlockSpec((tm, D_MODEL), lambda i, j: (i, 0)),
            pl.BlockSpec((1, D_MODEL), lambda i, j: (0, 0)),
            pl.BlockSpec((D_MODEL, tn), lambda i, j: (0, j)),
        ],
        out_specs=pl.BlockSpec((tm, tn), lambda i, j: (i, j)),
        out_shape=jax.ShapeDtypeStruct((n, PCOLS), F32),
        scratch_shapes=[pltpu.VMEM((tm, D_MODEL), BF16)],
        compiler_params=_cparams(("arbitrary", "arbitrary")),
        name="inproj",
    )(x2d, g, w_bf16)


def _s5_body(u_ref, bcat_ref, sc_ref, ccat_ref, d_ref, wglu_ref, bglu_ref, gain_ref, o_ref,
             hre_ref, him_ref, cre_ref, cim_ref):
    t = pl.program_id(1)
    tt = u_ref.shape[1]

    @pl.when(t == 0)
    def _():
        cre_ref[...] = jnp.zeros_like(cre_ref)
        cim_ref[...] = jnp.zeros_like(cim_ref)

    u = u_ref[0]
    bu = jnp.dot(u.astype(BF16), bcat_ref[...], preferred_element_type=F32)
    hre_ref[...] = bu[:, :S5_NST]
    him_ref[...] = bu[:, S5_NST:]

    def tile_step(j, carry):
        cr, ci = carry
        r0 = pl.multiple_of(j * SUBLANES, SUBLANES)
        ur = hre_ref[pl.ds(r0, SUBLANES), :]
        ui = him_ref[pl.ds(r0, SUBLANES), :]
        for k, d in enumerate((1, 2, 4)):
            ar = sc_ref[2 * k]
            ai = sc_ref[2 * k + 1]
            sr = pltpu.roll(ur, d, 0)
            si = pltpu.roll(ui, d, 0)
            ur, ui = ur + (ar * sr - ai * si), ui + (ar * si + ai * sr)
        pr = sc_ref[6]
        pi = sc_ref[7]
        hr = ur + (pr * cr - pi * ci)
        hi = ui + (pr * ci + pi * cr)
        hre_ref[pl.ds(r0, SUBLANES), :] = hr
        him_ref[pl.ds(r0, SUBLANES), :] = hi
        return hr[SUBLANES - 1:SUBLANES, :], hi[SUBLANES - 1:SUBLANES, :]

    cr, ci = lax.fori_loop(0, tt // SUBLANES, tile_step,
                           (cre_ref[SUBLANES - 1:SUBLANES, :], cim_ref[SUBLANES - 1:SUBLANES, :]))
    cre_ref[...] = jnp.broadcast_to(cr, cre_ref.shape)
    cim_ref[...] = jnp.broadcast_to(ci, cim_ref.shape)

    hcat = jnp.concatenate([hre_ref[...].astype(BF16), him_ref[...].astype(BF16)], axis=-1)
    y = jnp.dot(hcat, ccat_ref[...], preferred_element_type=F32) + d_ref[...] * u
    y = _gelu(y)
    y = y * _sigmoid(jnp.dot(y.astype(BF16), wglu_ref[...], preferred_element_type=F32) + bglu_ref[...])
    o_ref[0] = _rms(y, gain_ref[...])


def _s5_params(lam_re, lam_im, b_re, b_im, c_re, c_im, log_dt):
    dt = jnp.exp(log_dt)[:, None]
    mag = jnp.exp(lam_re * dt)
    abar_re, abar_im = mag * jnp.cos(lam_im * dt), mag * jnp.sin(lam_im * dt)
    den = lam_re * lam_re + lam_im * lam_im
    q_re = ((abar_re - 1.0) * lam_re + abar_im * lam_im) / den
    q_im = (abar_im * lam_re - (abar_re - 1.0) * lam_im) / den
    bbar_re = q_re[..., None] * b_re - q_im[..., None] * b_im
    bbar_im = q_re[..., None] * b_im + q_im[..., None] * b_re
    eye = jnp.eye(S5_GROUPS, dtype=F32)

    def in_blockdiag(m):
        return jnp.einsum('gnh,gk->ghkn', m, eye).reshape(S5_GROUPS * S5_CH, S5_NST)

    def out_blockdiag(m):
        return jnp.einsum('ghn,gk->gnkh', m, eye).reshape(S5_NST, S5_GROUPS * S5_CH)

    bcat = jnp.concatenate([in_blockdiag(bbar_re), in_blockdiag(bbar_im)], axis=1).astype(BF16)
    ccat = jnp.concatenate([out_blockdiag(c_re), -out_blockdiag(c_im)], axis=0).astype(BF16)

    ar = abar_re.reshape(1, S5_NST)
    ai = abar_im.reshape(1, S5_NST)
    pows = [(ar, ai)]
    for _ in range(SUBLANES - 1):
        pr, pi = pows[-1]
        pows.append((pr * ar - pi * ai, pr * ai + pi * ar))
    row = jnp.arange(SUBLANES)[:, None]
    tiles = []
    for d in (1, 2, 4):
        pr, pi = pows[d - 1]
        tiles.append(jnp.where(row >= d, pr, 0.0))
        tiles.append(jnp.where(row >= d, pi, 0.0))
    tiles.append(jnp.concatenate([p[0] for p in pows], axis=0))
    tiles.append(jnp.concatenate([p[1] for p in pows], axis=0))
    sc = jnp.stack(tiles, axis=0)
    return bcat, ccat, sc


def _s5(proj3, bcat, sc, ccat, d_skip, w_glu, b_glu, gain, tt):
    bsz, s, _ = proj3.shape
    const = lambda *shape: pl.BlockSpec(shape, lambda b, t: (0,) * len(shape))
    return pl.pallas_call(
        _s5_body,
        grid=(bsz, s // tt),
        in_specs=[
            pl.BlockSpec((1, tt, GROUP_W), lambda b, t: (b, t, COL_S5 // GROUP_W)),
            const(GROUP_W, 2 * S5_NST),
            const(8, SUBLANES, S5_NST),
            const(2 * S5_NST, GROUP_W),
            const(1, GROUP_W),
            const(GROUP_W, GROUP_W),
            const(1, GROUP_W),
            const(1, GROUP_W),
        ],
        out_specs=pl.BlockSpec((1, tt, GROUP_W), lambda b, t: (b, t, 0)),
        out_shape=jax.ShapeDtypeStruct((bsz, s, GROUP_W), F32),
        scratch_shapes=[pltpu.VMEM((tt, S5_NST), F32), pltpu.VMEM((tt, S5_NST), F32),
                        pltpu.VMEM((SUBLANES, S5_NST), F32), pltpu.VMEM((SUBLANES, S5_NST), F32)],
        compiler_params=_cparams(("arbitrary", "arbitrary")),
        name="s5",
    )(proj3, bcat, sc, ccat, d_skip, w_glu, b_glu, gain)


def _load_ext(ext_ref, x, first):
    tt = x.shape[0]

    @pl.when(first)
    def _():
        ext_ref[0:SUBLANES, :] = jnp.zeros((SUBLANES, x.shape[1]), F32)

    @pl.when(jnp.logical_not(first))
    def _():
        ext_ref[0:SUBLANES, :] = ext_ref[tt:tt + SUBLANES, :]

    ext_ref[SUBLANES:SUBLANES + tt, :] = x


def _causal_conv(ext_ref, cw_ref, cb_ref, tt):
    y = cb_ref[...] + cw_ref[CONV_K - 1:CONV_K, :] * ext_ref[SUBLANES:SUBLANES + tt, :]
    for i in range(CONV_K - 1):
        off = SUBLANES - (CONV_K - 1) + i
        y = y + cw_ref[i:i + 1, :] * ext_ref[off:off + tt, :]
    return y


def _lru_body(x_ref, gate_ref, cw_ref, cb_ref, wa_ref, ba_ref, wx_ref, bx_ref, ap_ref, gain_ref, o_ref,
              ext_ref, a_ref, u_ref, c_ref):
    t = pl.program_id(1)
    tt = x_ref.shape[1]

    @pl.when(t == 0)
    def _():
        c_ref[...] = jnp.zeros_like(c_ref)

    _load_ext(ext_ref, x_ref[0], t == 0)
    xc = _causal_conv(ext_ref, cw_ref, cb_ref, tt)
    xcb = xc.astype(BF16)
    r = _sigmoid(jnp.dot(xcb, wa_ref[...], preferred_element_type=F32) + ba_ref[...])
    i = _sigmoid(jnp.dot(xcb, wx_ref[...], preferred_element_type=F32) + bx_ref[...])
    log_a = -LRU_C * r * _softplus(-ap_ref[...])
    a_ref[...] = jnp.exp(log_a)
    th = jnp.tanh(log_a)
    u_ref[...] = jnp.sqrt(-2.0 * th / (1.0 - th)) * (i * xc)

    row = lax.broadcasted_iota(jnp.int32, (SUBLANES, GROUP_W), 0)

    def tile_step(j, c):
        r0 = pl.multiple_of(j * SUBLANES, SUBLANES)
        a = a_ref[pl.ds(r0, SUBLANES), :]
        u = u_ref[pl.ds(r0, SUBLANES), :]
        for d in (1, 2, 4):
            keep = row >= d
            a_sh = jnp.where(keep, pltpu.roll(a, d, 0), 1.0)
            u_sh = jnp.where(keep, pltpu.roll(u, d, 0), 0.0)
            u = u + a * u_sh
            a = a * a_sh
        h = u + a * c
        u_ref[pl.ds(r0, SUBLANES), :] = h
        return h[SUBLANES - 1:SUBLANES, :]

    c = lax.fori_loop(0, tt // SUBLANES, tile_step, c_ref[SUBLANES - 1:SUBLANES, :])
    c_ref[...] = jnp.broadcast_to(c, c_ref.shape)

    y = u_ref[...] * _gelu(gate_ref[0])
    o_ref[0] = _rms(y, gain_ref[...])


def _blockdiag(w):
    h, a, b = w.shape
    return jnp.einsum('hij,hk->hikj', w, jnp.eye(h, dtype=w.dtype)).reshape(h * a, h * b)


def _lru(proj3, cw, cb, wa_bd, ba, wx_bd, bx, a_param, gain, tt):
    bsz, s, _ = proj3.shape
    const = lambda *shape: pl.BlockSpec(shape, lambda b, t: (0,) * len(shape))
    return pl.pallas_call(
        _lru_body,
        grid=(bsz, s // tt),
        in_specs=[
            pl.BlockSpec((1, tt, GROUP_W), lambda b, t: (b, t, COL_LX // GROUP_W)),
            pl.BlockSpec((1, tt, GROUP_W), lambda b, t: (b, t, COL_LG // GROUP_W)),
            const(CONV_K, GROUP_W), const(1, GROUP_W),
            const(GROUP_W, GROUP_W), const(1, GROUP_W),
            const(GROUP_W, GROUP_W), const(1, GROUP_W),
            const(1, GROUP_W), const(1, GROUP_W),
        ],
        out_specs=pl.BlockSpec((1, tt, GROUP_W), lambda b, t: (b, t, 0)),
        out_shape=jax.ShapeDtypeStruct((bsz, s, GROUP_W), F32),
        scratch_shapes=[pltpu.VMEM((tt + SUBLANES, GROUP_W), F32), pltpu.VMEM((tt, GROUP_W), F32),
                        pltpu.VMEM((tt, GROUP_W), F32), pltpu.VMEM((SUBLANES, GROUP_W), F32)],
        compiler_params=_cparams(("arbitrary", "arbitrary")),
        name="rglru",
    )(proj3, proj3, cw, cb, wa_bd, ba, wx_bd, bx, a_param, gain)


def _pair_order(x):
    return jnp.concatenate([x[:, h * RW_HEAD_DIM:(h + 1) * RW_HEAD_DIM] for h in RW_PAIR_ORDER], axis=-1)


def _rwpre_body(r_ref, k_ref, v_ref, l_ref, mu_r, mu_k, mu_v, mu_l, w0_ref, w2_ref, a0_ref, a2_ref, g2_ref,
                kk_ref, ka_ref, rk_ref, hm_ref,
                or_ref, ow_ref, ok_ref, ov_ref, okk_ref, onb_ref, og_ref, obon_ref,
                er_ref, ek_ref, ev_ref, el_ref):
    t = pl.program_id(1)
    tt = r_ref.shape[1]
    first = t == 0

    def lerp(ext_ref, x_ref, mu_ref):
        x = x_ref[0]
        _load_ext(ext_ref, x, first)
        prev = ext_ref[SUBLANES - 1:SUBLANES - 1 + tt, :]
        return x + (prev - x) * mu_ref[...]

    r = lerp(er_ref, r_ref, mu_r)
    k = lerp(ek_ref, k_ref, mu_k)
    v = lerp(ev_ref, v_ref, mu_v)
    lo = lerp(el_ref, l_ref, mu_l)

    w_log = -_softplus(-(w0_ref[...] + _bdot(jnp.tanh(lo), w2_ref[...]))) - 0.5
    decay = jnp.exp(-jnp.exp(w_log))
    a = _sigmoid(a0_ref[...] + _bdot(lo, a2_ref[...]))
    g = _bdot(_sigmoid(lo), g2_ref[...])
    kk = k * kk_ref[...]
    ss = jnp.dot(_hilo(kk * kk), hm_ref[...], preferred_element_type=F32)
    kk = kk * lax.rsqrt(jnp.maximum(ss, 1e-24))
    k2 = k * (1.0 + (a - 1.0) * ka_ref[...])
    srk = jnp.dot(_hilo(r * k2 * rk_ref[...]), hm_ref[...], preferred_element_type=F32)

    or_ref[0] = _pair_order(r)
    ow_ref[0] = _pair_order(decay)
    ok_ref[0] = _pair_order(k2)
    ov_ref[0] = _pair_order(v)
    okk_ref[0] = _pair_order(kk)
    onb_ref[0] = _pair_order(-(kk * a))
    og_ref[0] = g
    obon_ref[0] = srk * v


def _rwpre(proj3, mu_r, mu_k, mu_v, mu_l, w0, w2p, a0, a2p, g2p, k_k, k_a, r_k, hm2, tt):
    bsz, s, _ = proj3.shape
    const = lambda *shape: pl.BlockSpec(shape, lambda b, t: (0,) * len(shape))
    col = lambda c, w: pl.BlockSpec((1, tt, w), lambda b, t: (b, t, c // w))
    out = pl.BlockSpec((1, tt, GROUP_W), lambda b, t: (b, t, 0))
    sds = jax.ShapeDtypeStruct((bsz, s, GROUP_W), F32)
    return pl.pallas_call(
        _rwpre_body,
        grid=(bsz, s // tt),
        in_specs=[
            col(COL_R, GROUP_W), col(COL_K, GROUP_W), col(COL_V, GROUP_W), col(COL_LORA, RW_LORA_PAD),
            const(1, GROUP_W), const(1, GROUP_W), const(1, GROUP_W), const(1, RW_LORA_PAD),
            const(1, GROUP_W), const(RW_LORA_PAD, GROUP_W),
            const(1, GROUP_W), const(RW_LORA_PAD, GROUP_W), const(RW_LORA_PAD, GROUP_W),
            const(1, GROUP_W), const(1, GROUP_W), const(1, GROUP_W),
            const(2 * GROUP_W, GROUP_W),
        ],
        out_specs=[out] * 8,
        out_shape=[sds] * 8,
        scratch_shapes=[pltpu.VMEM((tt + SUBLANES, GROUP_W), F32)] * 3
                       + [pltpu.VMEM((tt + SUBLANES, RW_LORA_PAD), F32)],
        compiler_params=_cparams(("arbitrary", "arbitrary")),
        name="rwkv_pre",
    )(proj3, proj3, proj3, proj3, mu_r, mu_k, mu_v, mu_l, w0, w2p, a0, a2p, g2p, k_k, k_a, r_k, hm2)


def _rwscan_body(r_ref, w_ref, k_ref, v_ref, kk_ref, nb_ref, hmhm_ref, gg_ref, d2_ref, o_ref, s_ref):
    bsz = r_ref.shape[0]
    ngroups = r_ref.shape[1]
    npair = GROUP_W // LANES
    hd = RW_HEAD_DIM

    @pl.when(pl.program_id(0) == 0)
    def _():
        s_ref[...] = jnp.zeros_like(s_ref)

    hmhm = hmhm_ref[...]
    hm = hmhm[:LANES]
    gg = gg_ref[...]
    d2 = d2_ref[...]
    rowid = lax.broadcasted_iota(jnp.int32, (SUBLANES, GROUP_W), 0)
    nrow = bsz * npair * hd
    half = npair * hd

    def group(g, carry):
        tiles = [[ref[b, g] for b in range(bsz)] for ref in (r_ref, w_ref, k_ref, v_ref, kk_ref, nb_ref)]

        def rows(which, i):
            return jnp.concatenate(
                [jnp.broadcast_to(tiles[which][b][i:i + 1, q * LANES:(q + 1) * LANES], (hd, LANES))
                 for b in range(bsz) for q in range(npair)], axis=0)

        vcols = jnp.concatenate([(d2 * rows(3, i)).astype(BF16) for i in range(SUBLANES)], axis=0)
        ev_all = jnp.dot(vcols, hm, preferred_element_type=F32)
        state = s_ref[...]
        t2s = []
        for i in range(SUBLANES):
            sa = jnp.dot(_hilo(state * rows(4, i)), hmhm, preferred_element_type=F32)
            state = state * rows(1, i) + sa * rows(5, i) + ev_all[i * nrow:(i + 1) * nrow] * rows(2, i)
            t2s.append((state * rows(0, i)).astype(BF16))
        s_ref[...] = state
        o2 = lax.dot_general(gg, jnp.concatenate(t2s, axis=0), (((1,), (1,)), ((), ())),
                             preferred_element_type=F32)
        for b in range(bsz):
            out_tile = jnp.zeros((SUBLANES, GROUP_W), F32)
            for i in range(SUBLANES):
                base = i * nrow + b * half
                orow = jnp.concatenate([o2[0:1, base:base + half], o2[1:2, base:base + half]], axis=1)
                out_tile = jnp.where(rowid == i, jnp.broadcast_to(orow, (SUBLANES, GROUP_W)), out_tile)
            o_ref[b, g] = out_tile
        return carry

    lax.fori_loop(0, ngroups, group, 0)


def _rwscan(seqs, hmhm, gg, d2all, tt):
    bsz, s, _ = seqs[0].shape
    view = lambda x: x.reshape(bsz, s // SUBLANES, SUBLANES, GROUP_W)
    blk = pl.BlockSpec((bsz, tt // SUBLANES, SUBLANES, GROUP_W), lambda t: (0, t, 0, 0))
    const = lambda *shape: pl.BlockSpec(shape, lambda t: (0,) * len(shape))
    nrow = bsz * (GROUP_W // LANES) * RW_HEAD_DIM
    out = pl.pallas_call(
        _rwscan_body,
        grid=(s // tt,),
        in_specs=[blk] * 6 + [const(2 * LANES, LANES), const(SUBLANES, LANES), const(nrow, LANES)],
        out_specs=blk,
        out_shape=jax.ShapeDtypeStruct((bsz, s // SUBLANES, SUBLANES, GROUP_W), F32),
        scratch_shapes=[pltpu.VMEM((nrow, LANES), F32)],
        compiler_params=_cparams(("arbitrary",)),
        name="rwkv_scan",
    )(*[view(x) for x in seqs], hmhm, gg, d2all)
    return out.reshape(bsz, s, GROUP_W)


def _rwpost_body(o_ref, bon_ref, g_ref, lnw_ref, lnb_ref, hm_ref, y_ref):
    o = o_ref[0]
    inv = 1.0 / RW_HEAD_DIM
    mean = jnp.dot(_hilo(o), hm_ref[...], preferred_element_type=F32) * inv
    c = o - mean
    var = jnp.dot(_hilo(c * c), hm_ref[...], preferred_element_type=F32) * inv
    on = c * lax.rsqrt(var + RW_LN_EPS) * lnw_ref[...] + lnb_ref[...]
    y_ref[0] = (on + bon_ref[0]) * g_ref[0]


def _rwpost(o, bonus, g, ln_w, ln_b, hm2, tt):
    bsz, s, _ = o.shape
    blk = pl.BlockSpec((1, tt, GROUP_W), lambda b, t: (b, t, 0))
    const = lambda *shape: pl.BlockSpec(shape, lambda b, t: (0,) * len(shape))
    return pl.pallas_call(
        _rwpost_body,
        grid=(bsz, s // tt),
        in_specs=[blk, blk, blk, const(1, GROUP_W), const(1, GROUP_W), const(2 * GROUP_W, GROUP_W)],
        out_specs=blk,
        out_shape=jax.ShapeDtypeStruct((bsz, s, GROUP_W), F32),
        compiler_params=_cparams(("arbitrary", "arbitrary")),
        name="rwkv_post",
    )(o, bonus, g, ln_w, ln_b, hm2)


def _ssd_body(z_ref, xbc_ref, dt_ref, cw_ref, cb_ref, dtb_ref, acoef_ref, eh_ref, dskip_ref, nw_ref, tril_ref,
              o_ref, ext_ref, ht_ref, yd_ref):
    t = pl.program_id(1)
    tt = SSD_CHUNK
    gw = SSD_STATE
    hpg = SSD_HEADS // SSD_GROUPS
    gl = hpg * SSD_HEAD_DIM

    @pl.when(t == 0)
    def _():
        ht_ref[...] = jnp.zeros_like(ht_ref)

    _load_ext(ext_ref, xbc_ref[0], t == 0)
    xbc = _silu(_causal_conv(ext_ref, cw_ref, cb_ref, tt))
    xs = xbc[:, :GROUP_W]
    bm = xbc[:, GROUP_W:GROUP_W + SSD_GROUPS * gw]
    cm = xbc[:, GROUP_W + SSD_GROUPS * gw:]

    dt = _softplus(dt_ref[0] + dtb_ref[...])
    a = dt * acoef_ref[...]
    acs = jnp.dot(tril_ref[...], a, preferred_element_type=F32, precision=HIGHEST)
    acs_t = jnp.transpose(acs)
    acs_e = jnp.dot(acs, eh_ref[...], preferred_element_type=F32, precision=HIGHEST)
    dt_e = jnp.dot(dt, eh_ref[...], preferred_element_type=F32, precision=HIGHEST)
    xdt = xs * dt_e
    last = acs_e[tt - 1:tt, :]
    xdtw = xdt * jnp.exp(last - acs_e)
    causal = (lax.broadcasted_iota(jnp.int32, (tt, tt), 0) >= lax.broadcasted_iota(jnp.int32, (tt, tt), 1))

    y_off = []
    for g in range(SSD_GROUPS):
        bg = bm[:, g * gw:(g + 1) * gw]
        cg = cm[:, g * gw:(g + 1) * gw]
        scores = lax.dot_general(cg.astype(BF16), bg.astype(BF16), (((1,), (1,)), ((), ())),
                                 preferred_element_type=F32)
        for hh in range(hpg):
            h = g * hpg + hh
            seg = acs[:, h:h + 1] - acs_t[h:h + 1, :]
            dec = jnp.where(causal, jnp.exp(jnp.where(causal, seg, 0.0)), 0.0)
            yd_ref[:, h * SSD_HEAD_DIM:(h + 1) * SSD_HEAD_DIM] = _bdot(
                scores * dec, xdt[:, h * SSD_HEAD_DIM:(h + 1) * SSD_HEAD_DIM])
        h_in = ht_ref[g]
        y_off.append(_bdot(cg, h_in))
        st = _bdot(jnp.transpose(bg), xdtw[:, g * gl:(g + 1) * gl])
        ht_ref[g] = h_in * jnp.exp(last[:, g * gl:(g + 1) * gl]) + st

    y = yd_ref[...] + jnp.concatenate(y_off, axis=-1) * jnp.exp(acs_e) + xs * dskip_ref[...]
    y = y * _silu(z_ref[0])
    o_ref[0] = _rms(y, nw_ref[...])


def _ssd(proj3, cw, cb, dtb, acoef, eh, dskip_e, norm_w, tril):
    bsz, s, _ = proj3.shape
    tt = SSD_CHUNK
    const = lambda *shape: pl.BlockSpec(shape, lambda b, t: (0,) * len(shape))
    return pl.pallas_call(
        _ssd_body,
        grid=(bsz, s // tt),
        in_specs=[
            pl.BlockSpec((1, tt, GROUP_W), lambda b, t: (b, t, COL_Z // GROUP_W)),
            pl.BlockSpec((1, tt, SSD_XBC), lambda b, t: (b, t, COL_XBC // SSD_XBC)),
            pl.BlockSpec((1, tt, LANES), lambda b, t: (b, t, COL_DT // LANES)),
            const(CONV_K, SSD_XBC), const(1, SSD_XBC), const(1, LANES), const(1, LANES),
            const(LANES, GROUP_W), const(1, GROUP_W), const(1, GROUP_W), const(tt, tt),
        ],
        out_specs=pl.BlockSpec((1, tt, GROUP_W), lambda b, t: (b, t, 0)),
        out_shape=jax.ShapeDtypeStruct((bsz, s, GROUP_W), F32),
        scratch_shapes=[pltpu.VMEM((tt + SUBLANES, SSD_XBC), F32),
                        pltpu.VMEM((SSD_GROUPS, SSD_STATE, GROUP_W // SSD_GROUPS), F32),
                        pltpu.VMEM((tt, GROUP_W), F32)],
        compiler_params=_cparams(("arbitrary", "arbitrary")),
        name="ssd",
    )(proj3, proj3, proj3, cw, cb, dtb, acoef, eh, dskip_e, norm_w, tril)


def _outproj_body(y0_ref, y1_ref, y2_ref, y3_ref, x_ref, wo_ref, g_ref, wr_ref, br_ref, ltri_ref,
                  x1_ref, h_ref, eid_ref, gate_ref, cnt_ref, carry_ref):
    @pl.when(pl.program_id(0) == 0)
    def _():
        carry_ref[...] = jnp.zeros_like(carry_ref)

    acc = x_ref[...]
    for k, y_ref in enumerate((y0_ref, y1_ref, y2_ref, y3_ref)):
        acc = acc + jnp.dot(y_ref[...].astype(BF16), wo_ref[k * GROUP_W:(k + 1) * GROUP_W, :],
                            preferred_element_type=F32)
    x1_ref[...] = acc
    h = _rms(acc, g_ref[...])
    h_ref[...] = h

    h_hi = h.astype(BF16)
    h_lo = (h - h_hi.astype(F32)).astype(BF16)
    both = jnp.dot(h_hi, wr_ref[...], preferred_element_type=F32)
    logits = (both[:, :LANES] + both[:, LANES:]
              + jnp.dot(h_lo, wr_ref[:, :LANES], preferred_element_type=F32) + br_ref[...])
    lane = lax.broadcasted_iota(jnp.int32, logits.shape, 1)
    ninf = -jnp.inf
    big = jnp.int32(LANES)
    gl = jnp.where(lane < MOE_GROUPS, logits, ninf)
    gmax = jnp.max(gl, axis=-1, keepdims=True)
    gidx = jnp.min(jnp.where(gl == gmax, lane, big), axis=-1, keepdims=True)
    gprob = 1.0 / jnp.sum(jnp.exp(gl - gmax), axis=-1, keepdims=True)
    eidx = lane - MOE_GROUPS
    in_group = (eidx >= 0) & (eidx < MOE_EXPERTS) & ((eidx >> 3) == gidx)
    el = jnp.where(in_group, logits, ninf)
    v1 = jnp.max(el, axis=-1, keepdims=True)
    i1 = jnp.min(jnp.where(el == v1, lane, big), axis=-1, keepdims=True)
    el2 = jnp.where(lane == i1, ninf, el)
    v2 = jnp.max(el2, axis=-1, keepdims=True)
    i2 = jnp.min(jnp.where(el2 == v2, lane, big), axis=-1, keepdims=True)
    e2 = jnp.exp(v2 - v1)
    p1 = 1.0 / (1.0 + e2)
    p2 = e2 * p1
    gate_ref[...] = jnp.where(lane == 0, gprob * p1, jnp.where(lane == 1, gprob * p2, 0.0))

    e1 = i1 - MOE_GROUPS
    e2 = i2 - MOE_GROUPS
    oh1 = jnp.where(lane == e1, 1.0, 0.0)
    oh2 = jnp.where(lane == e2, 1.0, 0.0)
    both_oh = oh1 + oh2
    before = carry_ref[0:1, :] + jnp.dot(ltri_ref[...], both_oh.astype(BF16), preferred_element_type=F32)
    rank1 = jnp.sum(oh1 * before, axis=-1, keepdims=True).astype(jnp.int32)
    rank2 = jnp.sum(oh2 * before, axis=-1, keepdims=True).astype(jnp.int32)
    total = carry_ref[0:1, :] + jnp.sum(both_oh, axis=0, keepdims=True)
    carry_ref[...] = jnp.broadcast_to(total, carry_ref.shape)
    cnt_ref[...] = jnp.broadcast_to(total, cnt_ref.shape).astype(jnp.int32)
    eid_ref[...] = jnp.where(lane == 0, e1, jnp.where(lane == 1, e2,
                             jnp.where(lane == 2, rank1, jnp.where(lane == 3, rank2, 0))))


def _outproj(ys, x2d, wo_bf16, g, wr_hilo, br):
    n = x2d.shape[0]
    tm = min(256, n)
    yblk = pl.BlockSpec((tm, GROUP_W), lambda i: (i, 0))
    xblk = pl.BlockSpec((tm, D_MODEL), lambda i: (i, 0))
    rblk = pl.BlockSpec((tm, LANES), lambda i: (i, 0))
    const = lambda *shape: pl.BlockSpec(shape, lambda i: (0,) * len(shape))
    ltri = jnp.asarray(np.tril(np.ones((tm, tm), np.float32), -1), BF16)
    return pl.pallas_call(
        _outproj_body,
        grid=(n // tm,),
        in_specs=[yblk] * 4 + [xblk, const(D_MODEL, D_MODEL), const(1, D_MODEL), const(D_MODEL, 2 * LANES),
                               const(1, LANES), const(tm, tm)],
        out_specs=[xblk, xblk, rblk, rblk, const(SUBLANES, LANES)],
        out_shape=[jax.ShapeDtypeStruct((n, D_MODEL), F32), jax.ShapeDtypeStruct((n, D_MODEL), F32),
                   jax.ShapeDtypeStruct((n, LANES), jnp.int32), jax.ShapeDtypeStruct((n, LANES), F32),
                   jax.ShapeDtypeStruct((SUBLANES, LANES), jnp.int32)],
        scratch_shapes=[pltpu.VMEM((SUBLANES, LANES), F32)],
        compiler_params=_cparams(("arbitrary",)),
        name="outproj_router",
    )(*ys, x2d, wo_bf16, g, wr_hilo, br, ltri)


MOE_DMA_UNROLL = 8


def _moe_body(blk_e_ref, nused_ref, src_ref, dst_ref,
              h_hbm, w1_ref, w3_ref, w2_ref, y_hbm,
              xbuf, ybuf, gsem, ssem):
    i = pl.program_id(0)
    nused = nused_ref[0]
    slot = i % 2
    ndump = y_hbm.shape[0] - 2 * MOE_BLOCK

    def gather_copy(blk, sl, r):
        tok = src_ref[blk * MOE_BLOCK + r]
        return pltpu.make_async_copy(h_hbm.at[pl.ds(tok, 1), :], xbuf.at[sl, pl.ds(r, 1), :], gsem.at[sl])

    def scatter_copy(blk, sl, r):
        row = dst_ref[blk * MOE_BLOCK + r]
        return pltpu.make_async_copy(ybuf.at[sl, pl.ds(r, 1), :], y_hbm.at[pl.ds(row, 1), :], ssem.at[sl])

    def for_rows(fn):
        def body(r8, c):
            for u in range(MOE_DMA_UNROLL):
                fn(r8 * MOE_DMA_UNROLL + u, u)
            return c
        lax.fori_loop(0, MOE_BLOCK // MOE_DMA_UNROLL, body, 0)

    def start_gather(blk, sl):
        for_rows(lambda r, u: gather_copy(blk, sl, r).start(priority=u % 2))

    def wait_gather(blk, sl):
        for_rows(lambda r, u: gather_copy(blk, sl, r).wait())

    def start_scatter(blk, sl):
        for_rows(lambda r, u: scatter_copy(blk, sl, r).start(priority=u % 2))

    def wait_scatter(blk, sl):
        for_rows(lambda r, u: scatter_copy(blk, sl, r).wait())

    def dump_copy(sl):
        return pltpu.make_async_copy(ybuf.at[sl], y_hbm.at[pl.ds(ndump + sl * MOE_BLOCK, MOE_BLOCK), :],
                                     ssem.at[sl])

    @pl.when(i == 0)
    def _():
        ybuf[...] = jnp.zeros_like(ybuf)
        for sl in range(2):
            dump_copy(sl).start()
        for sl in range(2):
            dump_copy(sl).wait()

    @pl.when(i < nused)
    def _():
        @pl.when(i == 0)
        def _():
            start_gather(i, slot)

        @pl.when(i + 1 < nused)
        def _():
            start_gather(i + 1, 1 - slot)

        wait_gather(i, slot)
        xb = xbuf[slot].astype(BF16)
        h1 = jnp.dot(xb, w1_ref[0, 0].astype(BF16), preferred_element_type=F32)
        h3 = jnp.dot(xb, w3_ref[0, 0].astype(BF16), preferred_element_type=F32)
        act = (_silu(h1) * h3).astype(BF16)
        y = jnp.dot(act, w2_ref[0, 0].astype(BF16), preferred_element_type=F32)

        @pl.when(i >= 2)
        def _():
            wait_scatter(i - 2, slot)

        ybuf[slot] = y
        start_scatter(i, slot)

        @pl.when(i == nused - 1)
        def _():
            @pl.when(i >= 1)
            def _():
                wait_scatter(i - 1, 1 - slot)
            wait_scatter(i, slot)


def _moe_ffn(blk_e, nused, src_tok, dst_row, h2d, w1, w3, w2, layer, nrows_out):
    nb = blk_e.shape[0]
    wspec = lambda a, b: pl.BlockSpec((1, 1, a, b), lambda i, be, nu, s, d: (layer, be[i], 0, 0))
    grid_spec = pltpu.PrefetchScalarGridSpec(
        num_scalar_prefetch=4,
        grid=(nb,),
        in_specs=[
            pl.BlockSpec(memory_space=pl.ANY),
            wspec(D_MODEL, MOE_DFF), wspec(D_MODEL, MOE_DFF), wspec(MOE_DFF, D_MODEL),
        ],
        out_specs=pl.BlockSpec(memory_space=pl.ANY),
        scratch_shapes=[pltpu.VMEM((2, MOE_BLOCK, D_MODEL), F32), pltpu.VMEM((2, MOE_BLOCK, D_MODEL), F32),
                        pltpu.SemaphoreType.DMA((2,)), pltpu.SemaphoreType.DMA((2,))],
    )
    return pl.pallas_call(
        _moe_body,
        grid_spec=grid_spec,
        out_shape=jax.ShapeDtypeStruct((nrows_out, D_MODEL), F32),
        compiler_params=_cparams(("arbitrary",)),
        name="moe_ffn",
    )(blk_e, nused, src_tok, dst_row, h2d, w1, w3, w2)


def _dispatch(eid_rank, counts, n):
    na = n * MOE_TOPK
    eid = eid_rank[:, 0:MOE_TOPK]
    rank = eid_rank[:, MOE_TOPK:2 * MOE_TOPK]
    padded = (counts + MOE_BLOCK - 1) // MOE_BLOCK * MOE_BLOCK
    pends = jnp.cumsum(padded)
    pstarts = pends - padded
    onehot = eid[:, :, None] == jnp.arange(MOE_EXPERTS, dtype=jnp.int32)[None, None, :]
    dest = rank + jnp.sum(jnp.where(onehot, pstarts[None, None, :], 0), axis=-1)
    cap = -(-(na + MOE_EXPERTS * (MOE_BLOCK - 1)) // MOE_BLOCK) * MOE_BLOCK
    nb = cap // MOE_BLOCK
    inv = jnp.full((cap,), -1, jnp.int32).at[dest.reshape(na)].set(jnp.arange(na, dtype=jnp.int32))
    valid = inv >= 0
    a = jnp.maximum(inv, 0)
    pos = jnp.arange(cap, dtype=jnp.int32)
    dump = MOE_TOPK * n + ((pos // MOE_BLOCK) % 2) * MOE_BLOCK + pos % MOE_BLOCK
    src_tok = a // MOE_TOPK
    dst_row = jnp.where(valid, (a % MOE_TOPK) * n + a // MOE_TOPK, dump)
    blk_e = jnp.minimum(jnp.sum(pends[None, :] <= (jnp.arange(nb, dtype=jnp.int32) * MOE_BLOCK)[:, None], axis=1),
                        MOE_EXPERTS - 1).astype(jnp.int32)
    nused = (pends[-1] // MOE_BLOCK).astype(jnp.int32).reshape(1)
    return blk_e, nused, src_tok, dst_row


def _combine_body(x_ref, y0_ref, y1_ref, gate_ref, o_ref):
    gate = gate_ref[...]
    o_ref[...] = x_ref[...] + gate[:, 0:1] * y0_ref[...] + gate[:, 1:2] * y1_ref[...]


def _combine_norm_body(x_ref, y0_ref, y1_ref, gate_ref, g_ref, o_ref):
    gate = gate_ref[...]
    o_ref[...] = _rms(x_ref[...] + gate[:, 0:1] * y0_ref[...] + gate[:, 1:2] * y1_ref[...], g_ref[...])


def _combine(x2d, y, gate, g=None):
    n = x2d.shape[0]
    tm = min(256, n)
    xblk = pl.BlockSpec((tm, D_MODEL), lambda i: (i, 0))
    y1blk = pl.BlockSpec((tm, D_MODEL), lambda i: (i + n // tm, 0))
    gblk = pl.BlockSpec((tm, LANES), lambda i: (i, 0))
    if g is None:
        body, ins, args = _combine_body, [xblk, xblk, y1blk, gblk], (x2d, y, y, gate)
    else:
        body, ins, args = (_combine_norm_body, [xblk, xblk, y1blk, gblk, pl.BlockSpec((1, D_MODEL), lambda i: (0, 0))],
                           (x2d, y, y, gate, g))
    return pl.pallas_call(
        body, grid=(n // tm,), in_specs=ins, out_specs=xblk,
        out_shape=jax.ShapeDtypeStruct((n, D_MODEL), F32),
        compiler_params=_cparams(("arbitrary",)),
        name="combine",
    )(*args)


def _pad_cols(w, width):
    return jnp.pad(w, ((0, 0), (0, width - w.shape[1])))


def _layout_w_in(w):
    s5 = w[:, 0:512]
    lx = w[:, 512:1024]
    lg = w[:, 1024:1536]
    r = w[:, 1536:2048]
    k = w[:, 2048:2560]
    v = w[:, 2560:3072]
    lora = _pad_cols(w[:, 3072:3232], RW_LORA_PAD)
    z = w[:, 3232:3744]
    xbc = w[:, 3744:4768]
    dt = _pad_cols(w[:, 4768:4776], LANES)
    return jnp.concatenate([xbc, s5, lx, lg, z, r, k, v, lora, dt], axis=1).astype(BF16)


def _head_ones(width, head):
    idx = np.arange(width) // head
    return (idx[:, None] == idx[None, :]).astype(np.float32)


def kernel(x, norm_mix, w_in, w_out, s5_lambda_re, s5_lambda_im, s5_b_re, s5_b_im, s5_c_re, s5_c_im, s5_d, s5_log_dt, s5_w_glu, s5_b_glu, s5_gain, lru_conv_w, lru_conv_b, lru_wa, lru_ba, lru_wx, lru_bx, lru_a_param, lru_gain, rw_mu, rw_w0, rw_w2, rw_a0, rw_a2, rw_g2, rw_k_k, rw_k_a, rw_r_k, rw_ln_w, rw_ln_b, ssd_conv_w, ssd_conv_b, ssd_dt_bias, ssd_a_log, ssd_d, ssd_norm_w, norm_ffn, moe_w_group, moe_b_group, moe_w_expert, moe_b_expert, moe_w1, moe_w3, moe_w2, norm_final):
    bsz, s, d = x.shape
    n = bsz * s
    tt = min(256, s)
    row = lambda v: v.reshape(1, -1).astype(F32)

    hm512 = _head_ones(GROUP_W, RW_HEAD_DIM)
    hm2 = jnp.asarray(np.concatenate([hm512, hm512], axis=0), BF16)
    hm128 = _head_ones(LANES, RW_HEAD_DIM)
    hmhm = jnp.asarray(np.concatenate([hm128, hm128], axis=0), BF16)
    gg_np = np.zeros((SUBLANES, LANES), np.float32)
    for j in range(2):
        gg_np[j, j * RW_HEAD_DIM:(j + 1) * RW_HEAD_DIM] = 1.0
    gg = jnp.asarray(gg_np, BF16)
    vv = np.arange(RW_HEAD_DIM)
    d2_np = (vv[:, None] == (np.arange(LANES) % RW_HEAD_DIM)[None, :]).astype(np.float32)
    d2all = jnp.asarray(np.tile(d2_np, (bsz * (GROUP_W // LANES), 1)), F32)
    eh_np = np.zeros((LANES, GROUP_W), np.float32)
    for h in range(SSD_HEADS):
        eh_np[h, h * SSD_HEAD_DIM:(h + 1) * SSD_HEAD_DIM] = 1.0
    eh = jnp.asarray(eh_np, F32)
    tril = jnp.asarray(np.tril(np.ones((SSD_CHUNK, SSD_CHUNK), np.float32)), F32)

    x2d = x.reshape(n, d)
    out = None
    for l in range(DEPTH):
        proj = _inproj(x2d, row(norm_mix[l]), _layout_w_in(w_in[l]))
        proj3 = proj.reshape(bsz, s, PCOLS)

        bcat, ccat, sc = _s5_params(s5_lambda_re[l], s5_lambda_im[l], s5_b_re[l], s5_b_im[l],
                                    s5_c_re[l], s5_c_im[l], s5_log_dt[l])
        y_s5 = _s5(proj3, bcat, sc, ccat, row(s5_d[l]), s5_w_glu[l].astype(BF16), row(s5_b_glu[l]),
                   row(s5_gain[l]), tt)

        y_lru = _lru(proj3, lru_conv_w[l], row(lru_conv_b[l]), _blockdiag(lru_wa[l]).astype(BF16), row(lru_ba[l]),
                     _blockdiag(lru_wx[l]).astype(BF16), row(lru_bx[l]), row(lru_a_param[l]), row(lru_gain[l]), tt)

        mu = rw_mu[l]
        mu_l = jnp.pad(mu[3 * GROUP_W:], (0, RW_LORA_PAD - (RW_DECAY_LORA + RW_AAA_LORA + RW_GATE_LORA)))
        w2p = jnp.zeros((RW_LORA_PAD, GROUP_W), F32).at[0:RW_DECAY_LORA].set(rw_w2[l]).astype(BF16)
        a2p = jnp.zeros((RW_LORA_PAD, GROUP_W), F32).at[RW_DECAY_LORA:RW_DECAY_LORA + RW_AAA_LORA].set(
            rw_a2[l]).astype(BF16)
        g_lo = RW_DECAY_LORA + RW_AAA_LORA
        g2p = jnp.zeros((RW_LORA_PAD, GROUP_W), F32).at[g_lo:g_lo + RW_GATE_LORA].set(rw_g2[l]).astype(BF16)
        seqs = _rwpre(proj3, row(mu[0:GROUP_W]), row(mu[GROUP_W:2 * GROUP_W]), row(mu[2 * GROUP_W:3 * GROUP_W]),
                      row(mu_l), row(rw_w0[l]), w2p, row(rw_a0[l]), a2p, g2p, row(rw_k_k[l]), row(rw_k_a[l]),
                      row(rw_r_k[l]), hm2, tt)
        o_rw = _rwscan(seqs[:6], hmhm, gg, d2all, tt)
        y_rw = _rwpost(o_rw, seqs[7], seqs[6], row(rw_ln_w[l]), row(rw_ln_b[l]), hm2, tt)

        dtb = jnp.pad(ssd_dt_bias[l], (0, LANES - SSD_HEADS)).reshape(1, LANES)
        acoef = jnp.pad(-jnp.exp(ssd_a_log[l]), (0, LANES - SSD_HEADS)).reshape(1, LANES)
        dskip_e = jnp.repeat(ssd_d[l], SSD_HEAD_DIM).reshape(1, GROUP_W)
        y_ssd = _ssd(proj3, ssd_conv_w[l], row(ssd_conv_b[l]), dtb, acoef, eh, dskip_e, row(ssd_norm_w[l]), tril)

        wr = _pad_cols(jnp.concatenate([moe_w_group[l], moe_w_expert[l]], axis=1), LANES)
        wr_hi = wr.astype(BF16)
        wr_hilo = jnp.concatenate([wr_hi, (wr - wr_hi.astype(F32)).astype(BF16)], axis=1)
        br = _pad_cols(jnp.concatenate([moe_b_group[l], moe_b_expert[l]]).reshape(1, -1), LANES)
        flat = lambda y: y.reshape(n, GROUP_W)
        x1, h2, eid_rank, gate, counts = _outproj([flat(y_s5), flat(y_lru), flat(y_rw), flat(y_ssd)], x2d,
                                                  w_out[l].astype(BF16), row(norm_ffn[l]), wr_hilo, br)

        blk_e, nused, src_tok, dst_row = _dispatch(eid_rank, counts[0, :MOE_EXPERTS], n)
        y = _moe_ffn(blk_e, nused, src_tok, dst_row, h2, moe_w1, moe_w3, moe_w2, l,
                     MOE_TOPK * n + 2 * MOE_BLOCK)
        if l + 1 < DEPTH:
            x2d = _combine(x1, y, gate)
        else:
            out = _combine(x1, y, gate, row(norm_final))
    return out.reshape(bsz, s, d)
```
